```python
import jax, jax.numpy as jnp
from jax import lax
import numpy as np

D_MODEL = 1024
BATCH = 8
SEQ = 8192
DEPTH = 2

CHUNK = 64
D_MIX = D_MODEL
EPS = 1e-6
NEG = -1e30

ATT_HEAD_DIM = 64
D_ATT = D_MIX // 4
ATT_HEADS = D_ATT // ATT_HEAD_DIM
ATT_LEFT_CHUNKS = 8
ATT_BAND = (ATT_LEFT_CHUNKS + 1) * CHUNK
MAX_REL = 128
N_REL = 2 * MAX_REL + 1

D_ML = D_MIX // 2
ML_HEADS = 4
ML_HEAD_DIM = D_ML // ML_HEADS
ML_CONV = 4

D_CONV = D_MIX - D_ATT - D_ML
CONV_WIDTH = 31

IN_SIZES = (D_ATT, D_ATT, D_ATT, D_ATT,
            D_ML, D_ML, D_ML, D_ML, D_ML,
            ML_HEADS, ML_HEADS,
            D_CONV, D_CONV, D_CONV)
D_IN = 4 * D_ATT + 5 * D_ML + 2 * ML_HEADS + 3 * D_CONV

kernel_name = "hybrid_chunk_attn_mlstm_conformer_conv"


def split_columns(h):
    parts = []
    off = 0
    for s in IN_SIZES:
        parts.append(h[..., off:off + s])
        off += s
    return parts


def rms_norm(x, g):
    xf = x.astype(jnp.float32)
    y = xf * lax.rsqrt(jnp.mean(xf * xf, axis=-1, keepdims=True) + EPS)
    return (y * g.astype(jnp.float32)).astype(x.dtype)


def layer_norm(x, g, b):
    xf = x.astype(jnp.float32)
    mu = jnp.mean(xf, axis=-1, keepdims=True)
    xc = xf - mu
    y = xc * lax.rsqrt(jnp.mean(xc * xc, axis=-1, keepdims=True) + EPS)
    return (y * g.astype(jnp.float32) + b.astype(jnp.float32)).astype(x.dtype)


def causal_depthwise_conv(x, w, b):
    width, ch = w.shape
    xp = jnp.pad(x, ((0, 0), (width - 1, 0), (0, 0)))
    out = lax.conv_general_dilated(
        xp, w[:, None, :].astype(x.dtype), window_strides=(1,), padding='VALID',
        dimension_numbers=('NWC', 'WIO', 'NWC'), feature_group_count=ch)
    return out + b.astype(x.dtype)


def chunk_band_attention(q, k, v, q_g, k_g, rel_bias):
    B, S, H, Dh = q.shape
    n_chunks = S // CHUNK
    pad = ATT_LEFT_CHUNKS * CHUNK
    q = rms_norm(q, q_g)
    k = rms_norm(k, k_g)
    kp = jnp.pad(k, ((0, 0), (pad, 0), (0, 0), (0, 0)))
    vp = jnp.pad(v, ((0, 0), (pad, 0), (0, 0), (0, 0)))
    rel = jnp.arange(CHUNK)[:, None] - jnp.arange(ATT_BAND)[None, :] + pad
    rel_idx = jnp.clip(rel, -MAX_REL, MAX_REL) + MAX_REL
    bias = rel_bias[:, rel_idx].astype(jnp.float32)
    scale = Dh ** -0.5
    qc = q.reshape(B, n_chunks, CHUNK, H, Dh).transpose(1, 0, 3, 2, 4)

    def one_chunk(args):
        c, qb = args
        start = c * CHUNK
        kb = lax.dynamic_slice_in_dim(kp, start, ATT_BAND, axis=1)
        vb = lax.dynamic_slice_in_dim(vp, start, ATT_BAND, axis=1)
        s = jnp.einsum('bhqd,bkhd->bhqk', qb, kb,
                       preferred_element_type=jnp.float32) * scale + bias
        key_pos = start - pad + jnp.arange(ATT_BAND)
        s = jnp.where(key_pos[None, None, None, :] >= 0, s, NEG)
        p = jax.nn.softmax(s, axis=-1).astype(vb.dtype)
        return jnp.einsum('bhqk,bkhd->bqhd', p, vb)

    out = lax.map(one_chunk, (jnp.arange(n_chunks), qc))
    return out.transpose(1, 0, 2, 3, 4).reshape(B, S, H * Dh)


def mlstm_chunkwise(q, k, v, i_pre, f_pre):
    B, S, H, D = q.shape
    n_chunks = S // CHUNK
    f32 = jnp.float32

    def chunks4(a):
        return a.astype(f32).reshape(B, n_chunks, CHUNK, H, D).transpose(1, 0, 3, 2, 4)

    def chunks3(a):
        return a.astype(f32).reshape(B, n_chunks, CHUNK, H).transpose(1, 0, 3, 2)

    qc = chunks4(q)
    kc = chunks4(k) * (D ** -0.5)
    vc = chunks4(v)
    ic = chunks3(i_pre)
    lfc = chunks3(jax.nn.log_sigmoid(f_pre.astype(f32)))
    causal = jnp.tril(jnp.ones((CHUNK, CHUNK), dtype=bool))

    def step(carry, xs):
        C, n, m = carry
        qb, kb, vb, ib, lfb = xs
        b = jnp.cumsum(lfb, axis=-1)
        log_d = b[..., :, None] - b[..., None, :] + ib[..., None, :]
        log_d = jnp.where(causal, log_d, NEG)
        inter = b + m[..., None]
        m_t = jnp.maximum(jnp.max(log_d, axis=-1), inter)
        d_mat = jnp.exp(log_d - m_t[..., None])
        inter_w = jnp.exp(inter - m_t)
        s = jnp.einsum('bhtd,bhsd->bhts', qb, kb) * d_mat
        num = (jnp.einsum('bhts,bhsd->bhtd', s, vb)
               + inter_w[..., None] * jnp.einsum('bhvk,bhtk->bhtv', C, qb))
        den = jnp.sum(s, axis=-1) + inter_w * jnp.einsum('bhk,bhtk->bht', n, qb)
        h = num / jnp.maximum(jnp.abs(den), jnp.exp(-m_t))[..., None]
        b_last = b[..., -1]
        w_log = b_last[..., None] - b + ib
        m_new = jnp.maximum(b_last + m, jnp.max(w_log, axis=-1))
        decay = jnp.exp(b_last + m - m_new)
        w = jnp.exp(w_log - m_new[..., None])
        C_new = decay[..., None, None] * C + jnp.einsum('bhs,bhsv,bhsk->bhvk', w, vb, kb)
        n_new = decay[..., None] * n + jnp.einsum('bhs,bhsk->bhk', w, kb)
        return (C_new, n_new, m_new), h

    init = (jnp.zeros((B, H, D, D), f32), jnp.zeros((B, H, D), f32), jnp.zeros((B, H), f32))
    _, hs = lax.scan(step, init, (qc, kc, vc, ic, lfc))
    return hs.transpose(1, 0, 3, 2, 4).reshape(B, S, H, D)


def setup_inputs(seed: int = 0) -> dict:
    key = jax.random.key(seed)
    ks = jax.random.split(key, 17)
    f32 = jnp.float32
    nrm = lambda k, shape: jax.random.normal(k, shape, f32)
    return {
        "x": nrm(ks[0], (BATCH, SEQ, D_MODEL)),
        "norm_g": 1.0 + 0.02 * nrm(ks[1], (DEPTH, D_MODEL)),
        "w_in": nrm(ks[2], (DEPTH, D_MODEL, D_IN)) * D_MODEL ** -0.5,
        "att_q_g": 1.0 + 0.02 * nrm(ks[3], (DEPTH, ATT_HEAD_DIM)),
        "att_k_g": 1.0 + 0.02 * nrm(ks[4], (DEPTH, ATT_HEAD_DIM)),
        "att_rel_bias": 0.1 * nrm(ks[5], (DEPTH, ATT_HEADS, N_REL)),
        "ml_qk_conv_w": nrm(ks[6], (DEPTH, ML_CONV, 2 * D_ML)) * ML_CONV ** -0.5,
        "ml_qk_conv_b": 0.02 * nrm(ks[7], (DEPTH, 2 * D_ML)),
        "ml_b_i": 0.1 * nrm(ks[8], (DEPTH, ML_HEADS)),
        "ml_b_f": jnp.linspace(3.0, 6.0, ML_HEADS, dtype=f32)[None, :]
                  + 0.1 * nrm(ks[9], (DEPTH, ML_HEADS)),
        "ml_out_g": 1.0 + 0.02 * nrm(ks[10], (DEPTH, D_ML)),
        "cv_dw_w": nrm(ks[11], (DEPTH, CONV_WIDTH, D_CONV)) * CONV_WIDTH ** -0.5,
        "cv_dw_b": 0.02 * nrm(ks[12], (DEPTH, D_CONV)),
        "cv_ln_g": 1.0 + 0.02 * nrm(ks[13], (DEPTH, D_CONV)),
        "cv_ln_b": 0.02 * nrm(ks[14], (DEPTH, D_CONV)),
        "w_out": nrm(ks[15], (DEPTH, D_MIX, D_MODEL)) * D_MIX ** -0.5,
    }


def reference(x, norm_g, w_in, att_q_g, att_k_g, att_rel_bias, ml_qk_conv_w, ml_qk_conv_b,
              ml_b_i, ml_b_f, ml_out_g, cv_dw_w, cv_dw_b, cv_ln_g, cv_ln_b, w_out):
    B, S, _ = x.shape
    for l in range(DEPTH):
        h = rms_norm(x, norm_g[l])
        proj = jnp.einsum('bsd,de->bse', h, w_in[l])
        (a_q, a_k, a_v, a_z, m_q, m_k, m_v, m_o, m_z, m_i, m_f,
         c_a, c_b, c_z) = split_columns(proj)

        att = chunk_band_attention(
            a_q.reshape(B, S, ATT_HEADS, ATT_HEAD_DIM),
            a_k.reshape(B, S, ATT_HEADS, ATT_HEAD_DIM),
            a_v.reshape(B, S, ATT_HEADS, ATT_HEAD_DIM),
            att_q_g[l], att_k_g[l], att_rel_bias[l])
        att = att * jax.nn.silu(a_z)

        qk = jax.nn.silu(causal_depthwise_conv(jnp.concatenate([m_q, m_k], axis=-1),
                                               ml_qk_conv_w[l], ml_qk_conv_b[l]))
        m_q, m_k = qk[..., :D_ML], qk[..., D_ML:]
        i_pre = m_i.astype(jnp.float32) + ml_b_i[l].astype(jnp.float32)
        f_pre = m_f.astype(jnp.float32) + ml_b_f[l].astype(jnp.float32)
        hm = mlstm_chunkwise(
            m_q.reshape(B, S, ML_HEADS, ML_HEAD_DIM),
            m_k.reshape(B, S, ML_HEADS, ML_HEAD_DIM),
            m_v.reshape(B, S, ML_HEADS, ML_HEAD_DIM), i_pre, f_pre)
        hm = jax.nn.sigmoid(m_o.astype(jnp.float32)).reshape(B, S, ML_HEADS, ML_HEAD_DIM) * hm
        hm = rms_norm(hm, ml_out_g[l].reshape(ML_HEADS, ML_HEAD_DIM).astype(jnp.float32))
        ml = hm.reshape(B, S, D_ML).astype(x.dtype) * jax.nn.silu(m_z)

        u = c_a * jax.nn.sigmoid(c_b)
        u = causal_depthwise_conv(u, cv_dw_w[l], cv_dw_b[l])
        u = jax.nn.silu(layer_norm(u, cv_ln_g[l], cv_ln_b[l]))
        cv = u * jax.nn.silu(c_z)

        mixed = jnp.concatenate([att.astype(x.dtype), ml, cv.astype(x.dtype)], axis=-1)
        x = x + jnp.einsum('bse,ed->bsd', mixed, w_out[l])
    return x
```

```python
import functools

import jax
import jax.numpy as jnp
from jax import lax
from jax.experimental import pallas as pl
from jax.experimental.pallas import tpu as pltpu

F32 = jnp.float32
BF16 = jnp.bfloat16

CHUNK = 64
EPS = 1e-6
NEG = -1e30

D_MODEL = 1024
ATT_HEADS = 4
ATT_HEAD_DIM = 64
D_ATT = ATT_HEADS * ATT_HEAD_DIM
ATT_LEFT = 8 * CHUNK
ATT_BAND = ATT_LEFT + CHUNK
MAX_REL = 128
ML_HEADS = 4
ML_HEAD_DIM = 128
D_ML = ML_HEADS * ML_HEAD_DIM
ML_CONV = 4
D_CONV = 256
CONV_WIDTH = 31

C_AQ, C_AK, C_AV, C_AZ = 0, 256, 512, 768
C_MQ, C_MK, C_MV, C_MO, C_MZ = 1024, 1536, 2048, 2560, 3072
C_CA, C_CB, C_CZ = 3584, 3840, 4096
D_PROJ = 4352
X_ATT, X_ML, X_CV = 0, 256, 768

ROW_PAD = 8
U_PAD = 32
SEQ_TILE = 512
V7X_VMEM_LIMIT = 58 * 1024 * 1024


def _sigmoid(x):
    return 1.0 / (1.0 + jnp.exp(-x))


def _silu(x):
    return x * _sigmoid(x)


def _log_sigmoid(x):
    return jnp.minimum(x, 0.0) - jnp.log1p(jnp.exp(-jnp.abs(x)))


def _split_dot(x, w_bf16):
    hi = x.astype(BF16)
    lo = (x - hi.astype(F32)).astype(BF16)
    return (jnp.dot(hi, w_bf16, preferred_element_type=F32)
            + jnp.dot(lo, w_bf16, preferred_element_type=F32))


def _layer_kernel(x_ref, ng_ref, w_ref, wg_ref, gb_ref, aqg_ref, akg_ref, bias_ref,
                  cw_ref, cb_ref, mog_ref, dww_ref, dwb_ref, lng_ref, lnb_ref, wout_ref,
                  o_ref,
                  proj, hb, qn, kbuf, vbuf, mixed, extu, ct, nst, mst, *, tile):
    t = pl.program_id(1)
    n_chunks = tile // CHUNK

    @pl.when(t == 0)
    def _reset():
        proj[0:ROW_PAD, :] = jnp.zeros((ROW_PAD, D_PROJ), F32)
        hb[tile:tile + CHUNK, :] = jnp.zeros((CHUNK, D_MODEL), BF16)
        kbuf[0:ATT_LEFT, :] = jnp.zeros((ATT_LEFT, D_ATT), BF16)
        vbuf[0:ATT_LEFT, :] = jnp.zeros((ATT_LEFT, D_ATT), BF16)
        extu[0:U_PAD, :] = jnp.zeros((U_PAD, D_CONV), F32)
        ct[...] = jnp.zeros(ct.shape, F32)
        nst[...] = jnp.zeros(nst.shape, F32)
        mst[...] = jnp.zeros(mst.shape, F32)

    x = x_ref[...]
    ms = jnp.mean(x * x, axis=-1, keepdims=True)
    hb[0:tile, :] = (x * lax.rsqrt(ms + EPS) * ng_ref[...]).astype(BF16)
    proj[ROW_PAD:ROW_PAD + tile, :] = jnp.dot(hb[0:tile, :], w_ref[...],
                                              preferred_element_type=F32)

    r_i = lax.broadcasted_iota(jnp.int32, (D_ATT, D_ATT), 0) // ATT_HEAD_DIM
    c_i = lax.broadcasted_iota(jnp.int32, (D_ATT, D_ATT), 1) // ATT_HEAD_DIM
    same_head = r_i == c_i
    head_mean = jnp.where(same_head, 1.0 / ATT_HEAD_DIM, 0.0).astype(BF16)

    q = proj[ROW_PAD:ROW_PAD + tile, C_AQ:C_AQ + D_ATT]
    q_ms = _split_dot(q * q, head_mean)
    qn[...] = q * lax.rsqrt(q_ms + EPS) * (aqg_ref[...] * ATT_HEAD_DIM ** -0.5)
    k = proj[ROW_PAD:ROW_PAD + tile, C_AK:C_AK + D_ATT]
    k_ms = _split_dot(k * k, head_mean)
    kbuf[ATT_LEFT:ATT_LEFT + tile, :] = (k * lax.rsqrt(k_ms + EPS) * akg_ref[...]).astype(BF16)
    vbuf[ATT_LEFT:ATT_LEFT + tile, :] = proj[ROW_PAD:ROW_PAD + tile, C_AV:C_AV + D_ATT].astype(BF16)

    c_a = proj[ROW_PAD:ROW_PAD + tile, C_CA:C_CA + D_CONV]
    c_b = proj[ROW_PAD:ROW_PAD + tile, C_CB:C_CB + D_CONV]
    extu[U_PAD:U_PAD + tile, :] = c_a * _sigmoid(c_b)

    key_col = lax.broadcasted_iota(jnp.int32, (1, ATT_BAND), 1)
    lane_att = lax.broadcasted_iota(jnp.int32, (CHUNK, D_ATT), 1)
    row64 = lax.broadcasted_iota(jnp.int32, (CHUNK, CHUNK), 0)
    col64 = lax.broadcasted_iota(jnp.int32, (CHUNK, CHUNK), 1)
    tril = row64 >= col64
    eye = row64 == col64
    lane128 = lax.broadcasted_iota(jnp.int32, (8, 128), 1)

    def chunk_body(c, carry):
        r0 = pl.multiple_of(c * CHUNK, CHUNK)
        rows = pl.ds(ROW_PAD + r0, CHUNK)

        qc = qn[pl.ds(r0, CHUNK), :]
        q_bd = jnp.where(same_head, jnp.concatenate([qc] * ATT_HEADS, axis=0), 0.0).astype(BF16)
        k_band = kbuf[pl.ds(r0, ATT_BAND), :]
        v_band = vbuf[pl.ds(r0, ATT_BAND), :]
        s = lax.dot_general(q_bd, k_band, (((1,), (1,)), ((), ())),
                            preferred_element_type=F32) + bias_ref[...]
        first_valid = ATT_LEFT - (t * tile + r0)
        s = jnp.where(key_col >= first_valid, s, NEG)
        s_max = jnp.max(s, axis=-1, keepdims=True)
        p = jnp.exp(s - s_max)
        p = p * (1.0 / jnp.sum(p, axis=-1, keepdims=True))
        o = jnp.dot(p.astype(BF16), v_band, preferred_element_type=F32)
        att = jnp.where(
            lane_att < ATT_HEAD_DIM, o[0:CHUNK],
            jnp.where(lane_att < 2 * ATT_HEAD_DIM, o[CHUNK:2 * CHUNK],
                      jnp.where(lane_att < 3 * ATT_HEAD_DIM, o[2 * CHUNK:3 * CHUNK],
                                o[3 * CHUNK:4 * CHUNK])))
        att = att * _silu(proj[rows, C_AZ:C_AZ + D_ATT])
        mixed[pl.ds(r0, CHUNK), X_ATT:X_ATT + D_ATT] = att.astype(BF16)

        g16 = lax.dot_general(wg_ref[...], hb[pl.ds(r0, 2 * CHUNK), :], (((1,), (1,)), ((), ())),
                              preferred_element_type=F32) + gb_ref[...]
        i8 = g16[0:8]
        lf8 = _log_sigmoid(g16[8:16])
        b8 = lf8
        for sh in (1, 2, 4, 8, 16, 32):
            b8 = b8 + jnp.where(lane128 >= sh, pltpu.roll(b8, sh, axis=1), 0.0)
        a8 = i8 - b8

        for h in range(ML_HEADS):
            hc = h * ML_HEAD_DIM
            lf_row = lf8[h:h + 1, 0:CHUNK]
            a_row = a8[h:h + 1, 0:CHUNK]
            m_prev = mst[h:h + 1, 0:1]

            b_col = jnp.sum(jnp.where(tril, lf_row, 0.0), axis=-1, keepdims=True)
            a_col = jnp.sum(jnp.where(eye, a_row, 0.0), axis=-1, keepdims=True)
            g_col = jnp.maximum(jnp.max(jnp.where(tril, a_row, NEG), axis=-1, keepdims=True),
                                m_prev)
            d_mat = jnp.where(tril, jnp.exp(a_row - g_col), 0.0)
            inter_w = jnp.exp(m_prev - g_col)

            wq = proj[pl.ds(r0, CHUNK + ROW_PAD), C_MQ + hc:C_MQ + hc + ML_HEAD_DIM]
            wk = proj[pl.ds(r0, CHUNK + ROW_PAD), C_MK + hc:C_MK + hc + ML_HEAD_DIM]
            q_acc = cb_ref[:, hc:hc + ML_HEAD_DIM]
            k_acc = cb_ref[:, D_ML + hc:D_ML + hc + ML_HEAD_DIM]
            for j in range(ML_CONV):
                off = ROW_PAD - (ML_CONV - 1) + j
                q_acc = q_acc + cw_ref[j:j + 1, hc:hc + ML_HEAD_DIM] * wq[off:off + CHUNK]
                k_acc = k_acc + cw_ref[j:j + 1, D_ML + hc:D_ML + hc + ML_HEAD_DIM] * wk[off:off + CHUNK]
            q_h = _silu(q_acc)
            k_h = _silu(k_acc) * ML_HEAD_DIM ** -0.5
            v_h = proj[rows, C_MV + hc:C_MV + hc + ML_HEAD_DIM]
            q_b = q_h.astype(BF16)
            k_b = k_h.astype(BF16)

            c_t = ct[h]
            n_row = nst[h:h + 1, :]
            s_m = lax.dot_general(q_b, k_b, (((1,), (1,)), ((), ())),
                                  preferred_element_type=F32) * d_mat
            num = (jnp.dot(s_m.astype(BF16), v_h.astype(BF16), preferred_element_type=F32)
                   + inter_w * jnp.dot(q_b, c_t.astype(BF16), preferred_element_type=F32))
            den = (jnp.sum(s_m, axis=-1, keepdims=True)
                   + inter_w * jnp.sum(q_h * n_row, axis=-1, keepdims=True))
            h_val = num * (1.0 / jnp.maximum(jnp.abs(den), jnp.exp(-(b_col + g_col))))

            g_last = g_col[CHUNK - 1:CHUNK, :]
            b_last = b_col[CHUNK - 1:CHUNK, :]
            decay = jnp.exp(m_prev - g_last)
            w_col = jnp.exp(a_col - g_last)
            ct[h] = decay * c_t + lax.dot_general(
                k_b, (w_col * v_h).astype(BF16), (((0,), (0,)), ((), ())),
                preferred_element_type=F32)
            nst[h:h + 1, :] = decay * n_row + jnp.sum(w_col * k_h, axis=0, keepdims=True)
            mst[h:h + 1, :] = jnp.broadcast_to(b_last + g_last, (1, 128))

            hm = _sigmoid(proj[rows, C_MO + hc:C_MO + hc + ML_HEAD_DIM]) * h_val
            hm = hm * lax.rsqrt(jnp.mean(hm * hm, axis=-1, keepdims=True) + EPS)
            hm = hm * mog_ref[:, hc:hc + ML_HEAD_DIM]
            ml = hm * _silu(proj[rows, C_MZ + hc:C_MZ + hc + ML_HEAD_DIM])
            mixed[pl.ds(r0, CHUNK), X_ML + hc:X_ML + hc + ML_HEAD_DIM] = ml.astype(BF16)

        uwin = extu[pl.ds(r0, CHUNK + U_PAD), :]
        acc = jnp.broadcast_to(dwb_ref[...], (CHUNK, D_CONV))
        for j in range(CONV_WIDTH):
            off = U_PAD - (CONV_WIDTH - 1) + j
            acc = acc + dww_ref[j:j + 1, :] * uwin[off:off + CHUNK]
        mu = jnp.mean(acc, axis=-1, keepdims=True)
        xc = acc - mu
        y = xc * lax.rsqrt(jnp.mean(xc * xc, axis=-1, keepdims=True) + EPS)
        y = _silu(y * lng_ref[...] + lnb_ref[...])
        cv = y * _silu(proj[rows, C_CZ:C_CZ + D_CONV])
        mixed[pl.ds(r0, CHUNK), X_CV:X_CV + D_CONV] = cv.astype(BF16)
        return carry

    lax.fori_loop(0, n_chunks, chunk_body, 0)

    o_ref[...] = x_ref[...] + jnp.dot(mixed[...], wout_ref[...], preferred_element_type=F32)

    proj[0:ROW_PAD, C_MQ:C_MV] = proj[tile:tile + ROW_PAD, C_MQ:C_MV]
    extu[0:U_PAD, :] = extu[tile:tile + U_PAD, :]
    kbuf[0:ATT_LEFT, :] = kbuf[tile:tile + ATT_LEFT, :]
    vbuf[0:ATT_LEFT, :] = vbuf[tile:tile + ATT_LEFT, :]


def _const_spec(shape):
    nd = len(shape)
    return pl.BlockSpec(shape, lambda b, t: (0,) * nd)


def _layer(x, ng, w, wg, gb, aqg, akg, bias, cw, cb, mog, dww, dwb, lng, lnb, wout, *, tile):
    B, S, D = x.shape
    assert D == D_MODEL and S % tile == 0 and tile % CHUNK == 0 and tile >= ATT_LEFT
    consts = (ng, w, wg, gb, aqg, akg, bias, cw, cb, mog, dww, dwb, lng, lnb, wout)
    x_spec = pl.BlockSpec((None, tile, D), lambda b, t: (b, t, 0))
    return pl.pallas_call(
        functools.partial(_layer_kernel, tile=tile),
        grid=(B, S // tile),
        in_specs=[x_spec] + [_const_spec(c.shape) for c in consts],
        out_specs=x_spec,
        out_shape=jax.ShapeDtypeStruct(x.shape, x.dtype),
        scratch_shapes=[
            pltpu.VMEM((ROW_PAD + tile, D_PROJ), F32),
            pltpu.VMEM((tile + CHUNK, D_MODEL), BF16),
            pltpu.VMEM((tile, D_ATT), F32),
            pltpu.VMEM((ATT_LEFT + tile, D_ATT), BF16),
            pltpu.VMEM((ATT_LEFT + tile, D_ATT), BF16),
            pltpu.VMEM((tile, D_MODEL), BF16),
            pltpu.VMEM((U_PAD + tile, D_CONV), F32),
            pltpu.VMEM((ML_HEADS, ML_HEAD_DIM, ML_HEAD_DIM), F32),
            pltpu.VMEM((8, ML_HEAD_DIM), F32),
            pltpu.VMEM((8, 128), F32),
        ],
        compiler_params=pltpu.CompilerParams(
            dimension_semantics=("arbitrary", "arbitrary"),
            vmem_limit_bytes=V7X_VMEM_LIMIT),
        name="hybrid_layer",
    )(x, *consts)


def _rel_bias_table(rel_bias):
    rel = jnp.arange(CHUNK)[:, None] - jnp.arange(ATT_BAND)[None, :] + ATT_LEFT
    idx = jnp.clip(rel, -MAX_REL, MAX_REL) + MAX_REL
    return rel_bias[:, idx].astype(F32).reshape(ATT_HEADS * CHUNK, ATT_BAND)


def kernel(x, norm_g, w_in, att_q_g, att_k_g, att_rel_bias, ml_qk_conv_w, ml_qk_conv_b, ml_b_i,
           ml_b_f, ml_out_g, cv_dw_w, cv_dw_b, cv_ln_g, cv_ln_b, w_out):
    depth = w_in.shape[0]
    tile = min(SEQ_TILE, x.shape[1])
    g0 = C_CA
    g1 = g0 + 2 * ML_HEADS
    for l in range(depth):
        w = jnp.concatenate([w_in[l][:, :g0], w_in[l][:, g1:]], axis=1).astype(BF16)
        wg = jnp.zeros((16, D_MODEL), F32)
        wg = wg.at[0:ML_HEADS].set(w_in[l][:, g0:g0 + ML_HEADS].T)
        wg = wg.at[8:8 + ML_HEADS].set(w_in[l][:, g0 + ML_HEADS:g1].T)
        gb = jnp.zeros((16, 128), F32)
        gb = gb.at[0:ML_HEADS].set(jnp.broadcast_to(ml_b_i[l][:, None], (ML_HEADS, 128)))
        gb = gb.at[8:8 + ML_HEADS].set(jnp.broadcast_to(ml_b_f[l][:, None], (ML_HEADS, 128)))
        dww = jnp.zeros((32, D_CONV), F32).at[0:CONV_WIDTH].set(cv_dw_w[l])
        x = _layer(
            x, norm_g[l][None, :], w, wg.astype(BF16), gb,
            jnp.tile(att_q_g[l], ATT_HEADS)[None, :], jnp.tile(att_k_g[l], ATT_HEADS)[None, :],
            _rel_bias_table(att_rel_bias[l]),
            jnp.zeros((8, 2 * D_ML), F32).at[0:ML_CONV].set(ml_qk_conv_w[l]),
            ml_qk_conv_b[l][None, :], ml_out_g[l][None, :],
            dww, cv_dw_b[l][None, :], cv_ln_g[l][None, :], cv_ln_b[l][None, :],
            w_out[l].astype(BF16), tile=tile)
    return x
```

```python
import functools

import jax
import jax.numpy as jnp
from jax import lax
from jax.experimental import pallas as pl
from jax.experimental.pallas import tpu as pltpu

F32 = jnp.float32
BF16 = jnp.bfloat16

CHUNK = 64
EPS = 1e-6
NEG = -1e30

D_MODEL = 1024
ATT_HEADS = 4
ATT_HEAD_DIM = 64
D_ATT = ATT_HEADS * ATT_HEAD_DIM
ATT_LEFT = 8 * CHUNK
ATT_BAND = ATT_LEFT + CHUNK
MAX_REL = 128
ML_HEADS = 4
ML_HEAD_DIM = 128
D_ML = ML_HEADS * ML_HEAD_DIM
ML_CONV = 4
D_CONV = 256
CONV_WIDTH = 31

C_AQ, C_AK, C_AV, C_AZ = 0, 256, 512, 768
C_MQ, C_MK, C_MV, C_MO, C_MZ = 1024, 1536, 2048, 2560, 3072
C_CA, C_CB, C_CZ = 3584, 3840, 4096
D_PROJ = 4352
PIECE = 256
X_ATT, X_ML, X_CV = 0, 256, 768

BLOCK = 2 * CHUNK
ROW_PAD = 8
U_PAD = 32
SEQ_TILE = 1024
V7X_VMEM_LIMIT = 58 * 1024 * 1024


def _sigmoid(x):
    return 0.5 * jnp.tanh(0.5 * x) + 0.5


def _silu(x):
    hx = 0.5 * x
    return hx * jnp.tanh(hx) + hx


def _log_sigmoid(x):
    return jnp.minimum(x, 0.0) - jnp.log1p(jnp.exp(-jnp.abs(x)))


def _split_dot(x, w_bf16):
    hi = x.astype(BF16)
    lo = (x - hi.astype(F32)).astype(BF16)
    return (jnp.dot(hi, w_bf16, preferred_element_type=F32)
            + jnp.dot(lo, w_bf16, preferred_element_type=F32))


def _aligned(v, m):
    return v if isinstance(v, int) else pl.multiple_of(v, m)


def _shift_rows(x, k):
    if k % x.shape[0] == 0:
        return x
    return pltpu.roll(x, x.shape[0] - k, axis=0)


def _layer_kernel(x_ref, xn_ref, ng_ref, w_ref, wg_ref, gb_ref, aqg_ref, akg_ref, bias_ref,
                  cw_ref, cb_ref, mog_ref, dww_ref, dwb_ref, lng_ref, lnb_ref, wout_ref,
                  o_ref,
                  proj0, proj1, gsc0, gsc1, qn0, qn1, kst0, kst1, vst0, vst1, ust0, ust1,
                  mixed0, mixed1, hbuf, kbuf, vbuf, extu, bias_c, ct, nst, mst, *, tile):
    b = pl.program_id(0)
    t = pl.program_id(1)
    n_blocks = tile // BLOCK
    proj = (proj0, proj1)
    gsc = (gsc0, gsc1)
    qn = (qn0, qn1)
    kst = (kst0, kst1)
    vst = (vst0, vst1)
    ust = (ust0, ust1)
    mixed = (mixed0, mixed1)

    r_i = lax.broadcasted_iota(jnp.int32, (D_ATT, D_ATT), 0) // ATT_HEAD_DIM
    c_i = lax.broadcasted_iota(jnp.int32, (D_ATT, D_ATT), 1) // ATT_HEAD_DIM
    same_head = r_i == c_i
    head_mean = jnp.where(same_head, 1.0 / ATT_HEAD_DIM, 0.0).astype(BF16)
    key_col = lax.broadcasted_iota(jnp.int32, (1, ATT_BAND), 1)
    lane_att = lax.broadcasted_iota(jnp.int32, (CHUNK, D_ATT), 1)
    row64 = lax.broadcasted_iota(jnp.int32, (CHUNK, CHUNK), 0)
    col64 = lax.broadcasted_iota(jnp.int32, (CHUNK, CHUNK), 1)
    tril = row64 >= col64
    eye = row64 == col64
    lane_in_chunk = lax.broadcasted_iota(jnp.int32, (8, BLOCK), 1) % CHUNK

    def stage_a_pieces(x, s):
        def begin():
            ms = jnp.mean(x * x, axis=-1, keepdims=True)
            hbuf[...] = (x * lax.rsqrt(ms + EPS) * ng_ref[...]).astype(BF16)
            proj[s][0:ROW_PAD, C_MQ:C_MV] = proj[1 - s][BLOCK:BLOCK + ROW_PAD, C_MQ:C_MV]

        def project(c0):
            c1 = min(c0 + PIECE, D_PROJ)
            proj[s][ROW_PAD:ROW_PAD + BLOCK, c0:c1] = jnp.dot(
                hbuf[...], w_ref[:, c0:c1], preferred_element_type=F32)

        def gates():
            g16 = lax.dot_general(wg_ref[...], hbuf[...], (((1,), (1,)), ((), ())),
                                  preferred_element_type=F32) + gb_ref[...]
            lf8 = _log_sigmoid(g16[8:16])
            b8 = lf8
            for sh in (1, 2, 4, 8, 16, 32):
                b8 = b8 + jnp.where(lane_in_chunk >= sh, pltpu.roll(b8, sh, axis=1), 0.0)
            gsc[s][0:8, :] = g16[0:8] - b8
            gsc[s][8:16, :] = lf8

        def qk_norm():
            rows = slice(ROW_PAD, ROW_PAD + BLOCK)
            q = proj[s][rows, C_AQ:C_AQ + D_ATT]
            q_ms = _split_dot(q * q, head_mean)
            qn[s][...] = q * lax.rsqrt(q_ms + EPS) * (aqg_ref[...] * ATT_HEAD_DIM ** -0.5)
            k = proj[s][rows, C_AK:C_AK + D_ATT]
            k_ms = _split_dot(k * k, head_mean)
            kst[s][...] = (k * lax.rsqrt(k_ms + EPS) * akg_ref[...]).astype(BF16)

        def stage_v():
            vst[s][...] = proj[s][ROW_PAD:ROW_PAD + BLOCK, C_AV:C_AV + D_ATT].astype(BF16)

        def stage_u():
            rows = slice(ROW_PAD, ROW_PAD + BLOCK)
            ust[s][...] = (proj[s][rows, C_CA:C_CA + D_CONV]
                           * _sigmoid(proj[s][rows, C_CB:C_CB + D_CONV]))

        def piece(c0, *extra):
            def run():
                project(c0)
                for f in extra:
                    f()
            return run

        extras = {C_AV: (gates,), C_AZ: (qk_norm, stage_v), D_PROJ - PIECE: (stage_u,)}
        return begin, [piece(c0, *extras.get(c0, ())) for c0 in range(0, D_PROJ, PIECE)]

    class Chunk:
        def __init__(self, p, s, j):
            self.s = s
            self.rr = j * CHUNK
            self.r0 = _aligned(p * BLOCK + self.rr, CHUNK)
            self.rows = slice(ROW_PAD + self.rr, ROW_PAD + self.rr + CHUNK)
            self.bias = bias_c.at[2 * s + j]
            self.heads = {}
            self.conv_acc = None

        def scores(self):
            s, rr = self.s, self.rr
            qc = qn[s][rr:rr + CHUNK, :]
            q_bd = jnp.where(same_head, jnp.concatenate([qc] * ATT_HEADS, axis=0), 0.0).astype(BF16)
            k_band = kbuf[pl.ds(self.r0, ATT_BAND), :]
            self.sc = lax.dot_general(q_bd, k_band, (((1,), (1,)), ((), ())),
                                      preferred_element_type=F32) + self.bias[...]

        def softmax_pv(self):
            s, rr = self.s, self.rr
            sc = self.sc
            pe = jnp.exp(sc - jnp.max(sc, axis=-1, keepdims=True))
            inv_l = 1.0 / jnp.sum(pe, axis=-1, keepdims=True)
            v_band = vbuf[pl.ds(self.r0, ATT_BAND), :]
            o = jnp.dot(pe.astype(BF16), v_band, preferred_element_type=F32) * inv_l
            att = jnp.where(
                lane_att < ATT_HEAD_DIM, o[0:CHUNK],
                jnp.where(lane_att < 2 * ATT_HEAD_DIM, o[CHUNK:2 * CHUNK],
                          jnp.where(lane_att < 3 * ATT_HEAD_DIM, o[2 * CHUNK:3 * CHUNK],
                                    o[3 * CHUNK:4 * CHUNK])))
            att = att * _silu(proj[s][self.rows, C_AZ:C_AZ + D_ATT])
            mixed[s][rr:rr + CHUNK, X_ATT:X_ATT + D_ATT] = att.astype(BF16)

        def ml_phase1(self, heads):
            s, rr = self.s, self.rr
            for h in heads:
                hc = h * ML_HEAD_DIM
                wq = proj[s][rr:rr + ROW_PAD + CHUNK, C_MQ + hc:C_MQ + hc + ML_HEAD_DIM]
                wk = proj[s][rr:rr + ROW_PAD + CHUNK, C_MK + hc:C_MK + hc + ML_HEAD_DIM]
                q_acc = cb_ref[:, hc:hc + ML_HEAD_DIM]
                k_acc = cb_ref[:, D_ML + hc:D_ML + hc + ML_HEAD_DIM]
                for jj in range(ML_CONV):
                    off = ROW_PAD - (ML_CONV - 1) + jj
                    q_acc = q_acc + (cw_ref[jj:jj + 1, hc:hc + ML_HEAD_DIM]
                                     * _shift_rows(wq, off)[0:CHUNK])
                    k_acc = k_acc + (cw_ref[jj:jj + 1, D_ML + hc:D_ML + hc + ML_HEAD_DIM]
                                     * _shift_rows(wk, off)[0:CHUNK])
                q_h = _silu(q_acc)
                k_h = _silu(k_acc) * ML_HEAD_DIM ** -0.5
                q_b = q_h.astype(BF16)
                k_b = k_h.astype(BF16)
                qk = lax.dot_general(q_b, k_b, (((1,), (1,)), ((), ())),
                                     preferred_element_type=F32)
                self.heads[h] = dict(hc=hc, q_h=q_h, k_h=k_h, q_b=q_b, k_b=k_b, qk=qk)

        def ml_phase2(self, heads):
            s, rr = self.s, self.rr
            a8 = gsc[s][0:8, :]
            lf8 = gsc[s][8:16, :]
            for h in heads:
                hd = self.heads[h]
                lf_row = lf8[h:h + 1, rr:rr + CHUNK]
                a_row = a8[h:h + 1, rr:rr + CHUNK]
                m_prev = mst[h:h + 1, 0:1]
                b_col = jnp.sum(jnp.where(tril, lf_row, 0.0), axis=-1, keepdims=True)
                a_col = jnp.sum(jnp.where(eye, a_row, 0.0), axis=-1, keepdims=True)
                g_col = jnp.maximum(
                    jnp.max(jnp.where(tril, a_row, NEG), axis=-1, keepdims=True), m_prev)
                d_mat = jnp.where(tril, jnp.exp(a_row - g_col), 0.0)
                mst[h:h + 1, :] = jnp.broadcast_to(
                    b_col[CHUNK - 1:CHUNK, :] + g_col[CHUNK - 1:CHUNK, :], (1, 128))
                v_h = proj[s][self.rows, C_MV + hd["hc"]:C_MV + hd["hc"] + ML_HEAD_DIM]
                s_m = hd["qk"] * d_mat
                sv = jnp.dot(s_m.astype(BF16), v_h.astype(BF16), preferred_element_type=F32)
                hd.update(m_prev=m_prev, b_col=b_col, a_col=a_col, g_col=g_col, v_h=v_h,
                          s_m=s_m, sv=sv)

        def ml_phase3(self, heads):
            s, rr = self.s, self.rr
            for h in heads:
                hd = self.heads[h]
                hc = hd["hc"]
                m_prev, g_col, b_col = hd["m_prev"], hd["g_col"], hd["b_col"]
                n_row = nst[h:h + 1, :]
                inter_w = jnp.exp(m_prev - g_col)
                q_c = jnp.dot(hd["q_b"], ct[h].astype(BF16), preferred_element_type=F32)
                num = hd["sv"] + inter_w * q_c
                den = (jnp.sum(hd["s_m"], axis=-1, keepdims=True)
                       + inter_w * jnp.sum(hd["q_h"] * n_row, axis=-1, keepdims=True))
                h_val = num * (1.0 / jnp.maximum(jnp.abs(den), jnp.exp(-(b_col + g_col))))

                g_last = g_col[CHUNK - 1:CHUNK, :]
                decay = jnp.exp(m_prev - g_last)
                w_col = jnp.exp(hd["a_col"] - g_last)
                ct[h] = decay * ct[h] + lax.dot_general(
                    hd["k_b"], (w_col * hd["v_h"]).astype(BF16), (((0,), (0,)), ((), ())),
                    preferred_element_type=F32)
                nst[h:h + 1, :] = decay * n_row + jnp.sum(w_col * hd["k_h"], axis=0, keepdims=True)

                hm = _sigmoid(proj[s][self.rows, C_MO + hc:C_MO + hc + ML_HEAD_DIM]) * h_val
                hm = hm * lax.rsqrt(jnp.mean(hm * hm, axis=-1, keepdims=True) + EPS)
                hm = hm * mog_ref[:, hc:hc + ML_HEAD_DIM]
                ml = hm * _silu(proj[s][self.rows, C_MZ + hc:C_MZ + hc + ML_HEAD_DIM])
                mixed[s][rr:rr + CHUNK, X_ML + hc:X_ML + hc + ML_HEAD_DIM] = ml.astype(BF16)

        def conv_taps(self, residues):
            acc = self.conv_acc
            if acc is None:
                acc = jnp.broadcast_to(dwb_ref[...], (CHUNK, D_CONV))
            uwin = extu[pl.ds(self.r0, CHUNK + U_PAD), :]
            first = U_PAD - (CONV_WIDTH - 1)
            for res in residues:
                shifted = _shift_rows(uwin, res)
                for o_ in range(first, U_PAD + 1):
                    if o_ % 8 == res:
                        tap = o_ - first
                        acc = acc + dww_ref[tap:tap + 1, :] * shifted[o_ - res:o_ - res + CHUNK]
            self.conv_acc = acc

        def conv_out(self):
            s, rr = self.s, self.rr
            acc = self.conv_acc
            mu = jnp.mean(acc, axis=-1, keepdims=True)
            xc = acc - mu
            y = xc * lax.rsqrt(jnp.mean(xc * xc, axis=-1, keepdims=True) + EPS)
            y = _silu(y * lng_ref[...] + lnb_ref[...])
            cv = y * _silu(proj[s][self.rows, C_CZ:C_CZ + D_CONV])
            mixed[s][rr:rr + CHUNK, X_CV:X_CV + D_CONV] = cv.astype(BF16)

    def commit_staged(p, s):
        kbuf[pl.ds(_aligned(ATT_LEFT + p * BLOCK, BLOCK), BLOCK), :] = kst[s][...]
        vbuf[pl.ds(_aligned(ATT_LEFT + p * BLOCK, BLOCK), BLOCK), :] = vst[s][...]
        extu[pl.ds(_aligned(U_PAD + p * BLOCK, U_PAD), BLOCK), :] = ust[s][...]

    def stage_c_pieces(p, s):
        rows = pl.ds(_aligned(p * BLOCK, BLOCK), BLOCK)

        def piece(c0):
            def run():
                o_ref[rows, c0:c0 + PIECE] = x_ref[rows, c0:c0 + PIECE] + jnp.dot(
                    mixed[s][...], wout_ref[:, c0:c0 + PIECE], preferred_element_type=F32)
            return run

        return [piece(c0) for c0 in range(0, D_MODEL, PIECE)]

    def maybe_set_pair_bias(p):
        @pl.when(t * tile + p * BLOCK <= ATT_LEFT)
        def _():
            for j in range(2 * BLOCK // CHUNK):
                first_valid = ATT_LEFT - (t * tile + p * BLOCK + j * CHUNK)
                bias_c[j] = bias_ref[...] + jnp.where(key_col >= first_valid, 0.0, NEG)

    @pl.when(jnp.logical_and(b == 0, t == 0))
    def _first_step():
        proj1[BLOCK:BLOCK + ROW_PAD, C_MQ:C_MV] = jnp.zeros((ROW_PAD, C_MV - C_MQ), F32)
        mixed1[...] = jnp.zeros(mixed1.shape, BF16)
        begin, pieces = stage_a_pieces(x_ref[0:BLOCK, :], 0)
        begin()
        for f in pieces:
            f()

    @pl.when(t == 0)
    def _reset():
        proj0[0:ROW_PAD, C_MQ:C_MV] = jnp.zeros((ROW_PAD, C_MV - C_MQ), F32)
        kbuf[0:ATT_LEFT, :] = jnp.zeros((ATT_LEFT, D_ATT), BF16)
        vbuf[0:ATT_LEFT, :] = jnp.zeros((ATT_LEFT, D_ATT), BF16)
        extu[0:U_PAD, :] = jnp.zeros((U_PAD, D_CONV), F32)
        ct[...] = jnp.zeros(ct.shape, F32)
        nst[...] = jnp.zeros(nst.shape, F32)
        mst[...] = jnp.zeros(mst.shape, F32)

    def half_step(p, s, x_next, p_out):
        begin, a = stage_a_pieces(x_next, 1 - s)
        fillers = stage_c_pieces(p_out, 1 - s) + a
        units = []
        for k in (Chunk(p, s, 0), Chunk(p, s, 1)):
            part = functools.partial
            units += [k.scores, part(k.ml_phase1, (0,)), part(k.ml_phase1, (1,)),
                      part(k.ml_phase1, (2,)), part(k.ml_phase1, (3,)), part(k.conv_taps, (0, 1)),
                      k.softmax_pv, part(k.conv_taps, (2, 3)), part(k.ml_phase2, (0, 1)),
                      part(k.ml_phase2, (2, 3)), part(k.conv_taps, (4, 5)),
                      part(k.ml_phase3, (0, 1)), part(k.ml_phase3, (2, 3)),
                      part(k.conv_taps, (6, 7)), k.conv_out]
        begin()
        commit_staged(p, s)
        fillers.pop(0)()
        n_f = len(fillers)
        for i, unit in enumerate(units):
            unit()
            while fillers and len(fillers) > n_f - (i + 1) * n_f // len(units):
                fillers.pop(0)()

    def block_rows(q):
        return x_ref[pl.ds(_aligned(q * BLOCK, BLOCK), BLOCK), :]

    def body(i, carry):
        p = 2 * i
        maybe_set_pair_bias(p)
        half_step(p, 0, block_rows(p + 1), jnp.maximum(p - 1, 0))
        half_step(p + 1, 1, block_rows(p + 2), p)
        return carry

    lax.fori_loop(0, n_blocks // 2 - 1, body, 0)
    maybe_set_pair_bias(n_blocks - 2)
    half_step(n_blocks - 2, 0, block_rows(n_blocks - 1), n_blocks - 3)
    half_step(n_blocks - 1, 1, xn_ref[...], n_blocks - 2)
    for f in stage_c_pieces(n_blocks - 1, 1):
        f()

    extu[0:U_PAD, :] = extu[tile:tile + U_PAD, :]
    kbuf[0:ATT_LEFT, :] = kbuf[tile:tile + ATT_LEFT, :]
    vbuf[0:ATT_LEFT, :] = vbuf[tile:tile + ATT_LEFT, :]


def _const_spec(shape):
    nd = len(shape)
    return pl.BlockSpec(shape, lambda b, t: (0,) * nd)


def _layer(x, ng, w, wg, gb, aqg, akg, bias, cw, cb, mog, dww, dwb, lng, lnb, wout, *, tile):
    B, S, D = x.shape
    assert D == D_MODEL and S % tile == 0 and tile % (2 * BLOCK) == 0 and tile >= ATT_LEFT + BLOCK
    n_tiles = S // tile
    blocks_per_tile = tile // BLOCK
    consts = (ng, w, wg, gb, aqg, akg, bias, cw, cb, mog, dww, dwb, lng, lnb, wout)
    x_spec = pl.BlockSpec((None, tile, D), lambda b, t: (b, t, 0))

    def next_block(b, t):
        flat = jnp.minimum(b * n_tiles + t + 1, B * n_tiles - 1)
        return (flat // n_tiles, (flat % n_tiles) * blocks_per_tile, 0)

    return pl.pallas_call(
        functools.partial(_layer_kernel, tile=tile),
        grid=(B, n_tiles),
        in_specs=[x_spec, pl.BlockSpec((None, BLOCK, D), next_block)]
        + [_const_spec(c.shape) for c in consts],
        out_specs=x_spec,
        out_shape=jax.ShapeDtypeStruct(x.shape, x.dtype),
        scratch_shapes=(
            [pltpu.VMEM((ROW_PAD + BLOCK, D_PROJ), F32)] * 2
            + [pltpu.VMEM((16, BLOCK), F32)] * 2
            + [pltpu.VMEM((BLOCK, D_ATT), F32)] * 2
            + [pltpu.VMEM((BLOCK, D_ATT), BF16)] * 4
            + [pltpu.VMEM((BLOCK, D_CONV), F32)] * 2
            + [pltpu.VMEM((BLOCK, D_MODEL), BF16)] * 2
            + [pltpu.VMEM((BLOCK, D_MODEL), BF16)]
            + [pltpu.VMEM((ATT_LEFT + tile, D_ATT), BF16)] * 2
            + [pltpu.VMEM((U_PAD + tile, D_CONV), F32),
               pltpu.VMEM((2 * BLOCK // CHUNK, ATT_HEADS * CHUNK, ATT_BAND), F32),
               pltpu.VMEM((ML_HEADS, ML_HEAD_DIM, ML_HEAD_DIM), F32),
               pltpu.VMEM((8, ML_HEAD_DIM), F32),
               pltpu.VMEM((8, 128), F32)]),
        compiler_params=pltpu.CompilerParams(
            dimension_semantics=("arbitrary", "arbitrary"),
            vmem_limit_bytes=V7X_VMEM_LIMIT),
        name="hybrid_layer",
    )(x, x, *consts)


def _rel_bias_table(rel_bias):
    heads = rel_bias.shape[0]
    n_far = ATT_LEFT - MAX_REL + CHUNK
    n_near = ATT_BAND + CHUNK - 1 - n_far
    far = jnp.broadcast_to(rel_bias[:, 2 * MAX_REL:], (heads, n_far))
    near = rel_bias[:, 2 * MAX_REL - n_near:2 * MAX_REL][:, ::-1]
    base = jnp.concatenate([far, near], axis=1)
    rows = [base[:, CHUNK - 1 - q:CHUNK - 1 - q + ATT_BAND] for q in range(CHUNK)]
    return jnp.stack(rows, axis=1).astype(F32).reshape(heads * CHUNK, ATT_BAND)


def kernel(x, norm_g, w_in, att_q_g, att_k_g, att_rel_bias, ml_qk_conv_w, ml_qk_conv_b, ml_b_i,
           ml_b_f, ml_out_g, cv_dw_w, cv_dw_b, cv_ln_g, cv_ln_b, w_out):
    depth = w_in.shape[0]
    tile = min(SEQ_TILE, x.shape[1])
    g0 = C_CA
    g1 = g0 + 2 * ML_HEADS
    for l in range(depth):
        w = jnp.concatenate([w_in[l][:, :g0], w_in[l][:, g1:]], axis=1).astype(BF16)
        wg = jnp.zeros((16, D_MODEL), F32)
        wg = wg.at[0:ML_HEADS].set(w_in[l][:, g0:g0 + ML_HEADS].T)
        wg = wg.at[8:8 + ML_HEADS].set(w_in[l][:, g0 + ML_HEADS:g1].T)
        gb = jnp.zeros((16, BLOCK), F32)
        gb = gb.at[0:ML_HEADS].set(jnp.broadcast_to(ml_b_i[l][:, None], (ML_HEADS, BLOCK)))
        gb = gb.at[8:8 + ML_HEADS].set(jnp.broadcast_to(ml_b_f[l][:, None], (ML_HEADS, BLOCK)))
        dww = jnp.zeros((32, D_CONV), F32).at[0:CONV_WIDTH].set(cv_dw_w[l])
        x = _layer(
            x, norm_g[l][None, :], w, wg.astype(BF16), gb,
            jnp.tile(att_q_g[l], ATT_HEADS)[None, :], jnp.tile(att_k_g[l], ATT_HEADS)[None, :],
            _rel_bias_table(att_rel_bias[l]),
            jnp.zeros((8, 2 * D_ML), F32).at[0:ML_CONV].set(ml_qk_conv_w[l]),
            ml_qk_conv_b[l][None, :], ml_out_g[l][None, :],
            dww, cv_dw_b[l][None, :], cv_ln_g[l][None, :], cv_ln_b[l][None, :],
            w_out[l].astype(BF16), tile=tile)
    return x
```

```python
import functools

import jax
import jax.numpy as jnp
from jax import lax
from jax.experimental import pallas as pl
from jax.experimental.pallas import tpu as pltpu

F32 = jnp.float32
BF16 = jnp.bfloat16

CHUNK = 64
EPS = 1e-6
NEG = -1e30
LOG2E = 1.4426950408889634

D_MODEL = 1024
ATT_HEADS = 4
ATT_HEAD_DIM = 64
D_ATT = ATT_HEADS * ATT_HEAD_DIM
ATT_LEFT = 8 * CHUNK
ATT_BAND = ATT_LEFT + CHUNK
MAX_REL = 128
ML_HEADS = 4
ML_HEAD_DIM = 128
D_ML = ML_HEADS * ML_HEAD_DIM
ML_CONV = 4
D_CONV = 256
CONV_WIDTH = 31

C_AQ, C_AK, C_AV, C_AZ = 0, 256, 512, 768
C_MQ, C_MK, C_MV, C_MO, C_MZ = 1024, 1536, 2048, 2560, 3072
C_CA, C_CB, C_CZ = 3584, 3840, 4096
D_PROJ = 4352
PIECE = 256
X_ATT, X_ML, X_CV = 0, 256, 768

BLOCK = 2 * CHUNK
ROW_PAD = 8
U_PAD = 32
SEQ_TILE = 1024
V7X_VMEM_LIMIT = 58 * 1024 * 1024


def _sigmoid(x):
    return 0.5 * jnp.tanh(0.5 * x) + 0.5


def _silu(x):
    hx = 0.5 * x
    return hx * jnp.tanh(hx) + hx


def _log_sigmoid(x):
    return jnp.minimum(x, 0.0) - jnp.log1p(jnp.exp(-jnp.abs(x)))


def _split_dot(x, w_bf16):
    hi = x.astype(BF16)
    lo = (x - hi.astype(F32)).astype(BF16)
    return (jnp.dot(hi, w_bf16, preferred_element_type=F32)
            + jnp.dot(lo, w_bf16, preferred_element_type=F32))


def _aligned(v, m):
    return v if isinstance(v, int) else pl.multiple_of(v, m)


def _shift_rows(x, k):
    if k % x.shape[0] == 0:
        return x
    return pltpu.roll(x, x.shape[0] - k, axis=0)


def _layer_kernel(x_ref, xn_ref, ng_ref, w_ref, wg_ref, gb_ref, aqg_ref, akg_ref, bias_ref,
                  cw_ref, cb_ref, mog_ref, dww_ref, dwb_ref, lng_ref, lnb_ref, wout_ref,
                  o_ref,
                  proj0, proj1, gsc0, gsc1, qn0, qn1, kst0, kst1, vst0, vst1, ust0, ust1,
                  mixed0, mixed1, hbuf, kbuf, vbuf, extu, bias_c, scb, ct, nst, mst, *, tile):
    b = pl.program_id(0)
    t = pl.program_id(1)
    n_blocks = tile // BLOCK
    proj = (proj0, proj1)
    gsc = (gsc0, gsc1)
    qn = (qn0, qn1)
    kst = (kst0, kst1)
    vst = (vst0, vst1)
    ust = (ust0, ust1)
    mixed = (mixed0, mixed1)

    r_i = lax.broadcasted_iota(jnp.int32, (D_ATT, D_ATT), 0) // ATT_HEAD_DIM
    c_i = lax.broadcasted_iota(jnp.int32, (D_ATT, D_ATT), 1) // ATT_HEAD_DIM
    same_head = r_i == c_i
    head_mean = jnp.where(same_head, 1.0 / ATT_HEAD_DIM, 0.0).astype(BF16)
    head_one = jnp.where(same_head, 1.0, 0.0).astype(BF16)
    key_col = lax.broadcasted_iota(jnp.int32, (1, ATT_BAND), 1)
    lane_att = lax.broadcasted_iota(jnp.int32, (CHUNK, D_ATT), 1)
    row64 = lax.broadcasted_iota(jnp.int32, (CHUNK, CHUNK), 0)
    col64 = lax.broadcasted_iota(jnp.int32, (CHUNK, CHUNK), 1)
    tril = row64 >= col64
    eye = row64 == col64
    lane_in_chunk = lax.broadcasted_iota(jnp.int32, (8, BLOCK), 1) % CHUNK

    def stage_a_pieces(x, s):
        def begin():
            ms = jnp.mean(x * x, axis=-1, keepdims=True)
            hbuf[...] = (x * lax.rsqrt(ms + EPS) * ng_ref[...]).astype(BF16)
            proj[s][0:ROW_PAD, C_MQ:C_MV] = proj[1 - s][BLOCK:BLOCK + ROW_PAD, C_MQ:C_MV]

        def project(c0):
            c1 = min(c0 + PIECE, D_PROJ)
            proj[s][ROW_PAD:ROW_PAD + BLOCK, c0:c1] = jnp.dot(
                hbuf[...], w_ref[:, c0:c1], preferred_element_type=F32)

        def gates():
            g16 = lax.dot_general(wg_ref[...], hbuf[...], (((1,), (1,)), ((), ())),
                                  preferred_element_type=F32) + gb_ref[...]
            lf8 = _log_sigmoid(g16[8:16])
            b8 = lf8
            for sh in (1, 2, 4, 8, 16, 32):
                b8 = b8 + jnp.where(lane_in_chunk >= sh, pltpu.roll(b8, sh, axis=1), 0.0)
            gsc[s][0:8, :] = g16[0:8] - b8
            gsc[s][8:16, :] = lf8

        def qk_norm():
            rows = slice(ROW_PAD, ROW_PAD + BLOCK)
            q = proj[s][rows, C_AQ:C_AQ + D_ATT]
            q_ms = _split_dot(q * q, head_mean)
            qn[s][...] = (q * lax.rsqrt(q_ms + EPS)
                          * (aqg_ref[...] * (ATT_HEAD_DIM ** -0.5 * LOG2E))).astype(BF16)
            k = proj[s][rows, C_AK:C_AK + D_ATT]
            k_ms = _split_dot(k * k, head_mean)
            kst[s][...] = (k * lax.rsqrt(k_ms + EPS) * akg_ref[...]).astype(BF16)

        def stage_v():
            vst[s][...] = proj[s][ROW_PAD:ROW_PAD + BLOCK, C_AV:C_AV + D_ATT].astype(BF16)

        def stage_u():
            rows = slice(ROW_PAD, ROW_PAD + BLOCK)
            ust[s][...] = (proj[s][rows, C_CA:C_CA + D_CONV]
                           * _sigmoid(proj[s][rows, C_CB:C_CB + D_CONV]))

        def piece(c0, *extra):
            def run():
                project(c0)
                for f in extra:
                    f()
            return run

        starts = list(range(0, D_PROJ, PIECE))
        extras = {min(c0 for c0 in starts if c0 >= C_AV): (gates,),
                  min(c0 for c0 in starts if c0 >= C_AZ): (qk_norm, stage_v),
                  starts[-1]: (stage_u,)}
        return begin, [piece(c0, *extras.get(c0, ())) for c0 in starts]

    class Chunk:
        def __init__(self, p, s, j):
            self.s = s
            self.rr = j * CHUNK
            self.r0 = _aligned(p * BLOCK + self.rr, CHUNK)
            self.rows = slice(ROW_PAD + self.rr, ROW_PAD + self.rr + CHUNK)
            self.bias = bias_c.at[2 * s + j]
            self.scr = scb.at[2 * s + j]
            self.heads = {}
            self.conv_acc = None

        def scores(self):
            s, rr = self.s, self.rr
            qc = qn[s][rr:rr + CHUNK, :]
            q_bd = jnp.concatenate([qc] * ATT_HEADS, axis=0) * head_one
            k_band = kbuf[pl.ds(self.r0, ATT_BAND), :]
            self.scr[...] = lax.dot_general(q_bd, k_band, (((1,), (1,)), ((), ())),
                                            preferred_element_type=F32) + self.bias[...]

        def softmax_pv(self):
            s, rr = self.s, self.rr
            sc = self.scr[...]
            pe = jnp.exp2(sc - jnp.max(sc, axis=-1, keepdims=True))
            inv_l = 1.0 / jnp.sum(pe, axis=-1, keepdims=True)
            v_band = vbuf[pl.ds(self.r0, ATT_BAND), :]
            o = jnp.dot(pe.astype(BF16), v_band, preferred_element_type=F32) * inv_l
            att = jnp.where(
                lane_att < ATT_HEAD_DIM, o[0:CHUNK],
                jnp.where(lane_att < 2 * ATT_HEAD_DIM, o[CHUNK:2 * CHUNK],
                          jnp.where(lane_att < 3 * ATT_HEAD_DIM, o[2 * CHUNK:3 * CHUNK],
                                    o[3 * CHUNK:4 * CHUNK])))
            att = att * _silu(proj[s][self.rows, C_AZ:C_AZ + D_ATT])
            mixed[s][rr:rr + CHUNK, X_ATT:X_ATT + D_ATT] = att.astype(BF16)

        def ml_phase1(self, heads):
            s, rr = self.s, self.rr
            for h in heads:
                hc = h * ML_HEAD_DIM
                wq = proj[s][rr:rr + ROW_PAD + CHUNK, C_MQ + hc:C_MQ + hc + ML_HEAD_DIM]
                wk = proj[s][rr:rr + ROW_PAD + CHUNK, C_MK + hc:C_MK + hc + ML_HEAD_DIM]
                q_acc = cb_ref[:, hc:hc + ML_HEAD_DIM]
                k_acc = cb_ref[:, D_ML + hc:D_ML + hc + ML_HEAD_DIM]
                for jj in range(ML_CONV):
                    off = ROW_PAD - (ML_CONV - 1) + jj
                    q_acc = q_acc + (cw_ref[jj:jj + 1, hc:hc + ML_HEAD_DIM]
                                     * _shift_rows(wq, off)[0:CHUNK])
                    k_acc = k_acc + (cw_ref[jj:jj + 1, D_ML + hc:D_ML + hc + ML_HEAD_DIM]
                                     * _shift_rows(wk, off)[0:CHUNK])
                q_h = _silu(q_acc)
                k_h = _silu(k_acc) * ML_HEAD_DIM ** -0.5
                q_b = q_h.astype(BF16)
                k_b = k_h.astype(BF16)
                qk = lax.dot_general(q_b, k_b, (((1,), (1,)), ((), ())),
                                     preferred_element_type=F32)
                self.heads[h] = dict(hc=hc, q_h=q_h, k_h=k_h, q_b=q_b, k_b=k_b, qk=qk)

        def ml_phase2(self, heads):
            s, rr = self.s, self.rr
            a8 = gsc[s][0:8, :]
            lf8 = gsc[s][8:16, :]
            for h in heads:
                hd = self.heads[h]
                lf_row = lf8[h:h + 1, rr:rr + CHUNK]
                a_row = a8[h:h + 1, rr:rr + CHUNK]
                m_prev = mst[h:h + 1, 0:1]
                b_col = jnp.sum(jnp.where(tril, lf_row, 0.0), axis=-1, keepdims=True)
                a_col = jnp.sum(jnp.where(eye, a_row, 0.0), axis=-1, keepdims=True)
                g_col = jnp.maximum(
                    jnp.max(jnp.where(tril, a_row, NEG), axis=-1, keepdims=True), m_prev)
                d_mat = jnp.where(tril, jnp.exp(a_row - g_col), 0.0)
                mst[h:h + 1, :] = jnp.broadcast_to(
                    b_col[CHUNK - 1:CHUNK, :] + g_col[CHUNK - 1:CHUNK, :], (1, 128))
                v_h = proj[s][self.rows, C_MV + hd["hc"]:C_MV + hd["hc"] + ML_HEAD_DIM]
                s_m = hd["qk"] * d_mat
                sv = jnp.dot(s_m.astype(BF16), v_h.astype(BF16), preferred_element_type=F32)
                hd.update(m_prev=m_prev, b_col=b_col, a_col=a_col, g_col=g_col, v_h=v_h,
                          s_m=s_m, sv=sv)

        def ml_phase3(self, heads):
            s, rr = self.s, self.rr
            for h in heads:
                hd = self.heads[h]
                hc = hd["hc"]
                m_prev, g_col, b_col = hd["m_prev"], hd["g_col"], hd["b_col"]
                n_row = nst[h:h + 1, :]
                inter_w = jnp.exp(m_prev - g_col)
                q_c = jnp.dot(hd["q_b"], ct[h].astype(BF16), preferred_element_type=F32)
                num = hd["sv"] + inter_w * q_c
                den = (jnp.sum(hd["s_m"], axis=-1, keepdims=True)
                       + inter_w * jnp.sum(hd["q_h"] * n_row, axis=-1, keepdims=True))
                h_val = num * (1.0 / jnp.maximum(jnp.abs(den), jnp.exp(-(b_col + g_col))))

                g_last = g_col[CHUNK - 1:CHUNK, :]
                decay = jnp.exp(m_prev - g_last)
                w_col = jnp.exp(hd["a_col"] - g_last)
                ct[h] = decay * ct[h] + lax.dot_general(
                    hd["k_b"], (w_col * hd["v_h"]).astype(BF16), (((0,), (0,)), ((), ())),
                    preferred_element_type=F32)
                nst[h:h + 1, :] = decay * n_row + jnp.sum(w_col * hd["k_h"], axis=0, keepdims=True)

                hm = _sigmoid(proj[s][self.rows, C_MO + hc:C_MO + hc + ML_HEAD_DIM]) * h_val
                hm = hm * lax.rsqrt(jnp.mean(hm * hm, axis=-1, keepdims=True) + EPS)
                hm = hm * mog_ref[:, hc:hc + ML_HEAD_DIM]
                ml = hm * _silu(proj[s][self.rows, C_MZ + hc:C_MZ + hc + ML_HEAD_DIM])
                mixed[s][rr:rr + CHUNK, X_ML + hc:X_ML + hc + ML_HEAD_DIM] = ml.astype(BF16)

        def conv_taps(self, residues):
            acc = self.conv_acc
            if acc is None:
                acc = jnp.broadcast_to(dwb_ref[...], (CHUNK, D_CONV))
            uwin = extu[pl.ds(self.r0, CHUNK + U_PAD), :]
            first = U_PAD - (CONV_WIDTH - 1)
            for res in residues:
                shifted = _shift_rows(uwin, res)
                for o_ in range(first, U_PAD + 1):
                    if o_ % 8 == res:
                        tap = o_ - first
                        acc = acc + dww_ref[tap:tap + 1, :] * shifted[o_ - res:o_ - res + CHUNK]
            self.conv_acc = acc

        def conv_out(self):
            s, rr = self.s, self.rr
            acc = self.conv_acc
            mu = jnp.mean(acc, axis=-1, keepdims=True)
            xc = acc - mu
            y = xc * lax.rsqrt(jnp.mean(xc * xc, axis=-1, keepdims=True) + EPS)
            y = _silu(y * lng_ref[...] + lnb_ref[...])
            cv = y * _silu(proj[s][self.rows, C_CZ:C_CZ + D_CONV])
            mixed[s][rr:rr + CHUNK, X_CV:X_CV + D_CONV] = cv.astype(BF16)

    def commit_staged(p, s):
        kbuf[pl.ds(_aligned(ATT_LEFT + p * BLOCK, BLOCK), BLOCK), :] = kst[s][...]
        vbuf[pl.ds(_aligned(ATT_LEFT + p * BLOCK, BLOCK), BLOCK), :] = vst[s][...]
        extu[pl.ds(_aligned(U_PAD + p * BLOCK, U_PAD), BLOCK), :] = ust[s][...]

    def stage_c_pieces(p, s):
        rows = pl.ds(_aligned(p * BLOCK, BLOCK), BLOCK)

        def piece(c0):
            def run():
                o_ref[rows, c0:c0 + PIECE] = x_ref[rows, c0:c0 + PIECE] + jnp.dot(
                    mixed[s][...], wout_ref[:, c0:c0 + PIECE], preferred_element_type=F32)
            return run

        return [piece(c0) for c0 in range(0, D_MODEL, PIECE)]

    def maybe_set_pair_bias(p):
        @pl.when(t * tile + p * BLOCK <= ATT_LEFT)
        def _():
            for j in range(2 * BLOCK // CHUNK):
                first_valid = ATT_LEFT - (t * tile + p * BLOCK + j * CHUNK)
                bias_c[j] = bias_ref[...] + jnp.where(key_col >= first_valid, 0.0, NEG)

    @pl.when(jnp.logical_and(b == 0, t == 0))
    def _first_step():
        proj1[BLOCK:BLOCK + ROW_PAD, C_MQ:C_MV] = jnp.zeros((ROW_PAD, C_MV - C_MQ), F32)
        mixed1[...] = jnp.zeros(mixed1.shape, BF16)
        begin, pieces = stage_a_pieces(x_ref[0:BLOCK, :], 0)
        begin()
        for f in pieces:
            f()

    @pl.when(t == 0)
    def _reset():
        proj0[0:ROW_PAD, C_MQ:C_MV] = jnp.zeros((ROW_PAD, C_MV - C_MQ), F32)
        kbuf[0:ATT_LEFT, :] = jnp.zeros((ATT_LEFT, D_ATT), BF16)
        vbuf[0:ATT_LEFT, :] = jnp.zeros((ATT_LEFT, D_ATT), BF16)
        extu[0:U_PAD, :] = jnp.zeros((U_PAD, D_CONV), F32)
        ct[...] = jnp.zeros(ct.shape, F32)
        nst[...] = jnp.zeros(nst.shape, F32)
        mst[...] = jnp.zeros(mst.shape, F32)

    def half_step(p, s, x_next, p_out):
        begin, a = stage_a_pieces(x_next, 1 - s)
        fillers = stage_c_pieces(p_out, 1 - s) + a
        units = []
        for k in (Chunk(p, s, 0), Chunk(p, s, 1)):
            part = functools.partial
            units += [k.scores, part(k.ml_phase1, (0,)), part(k.ml_phase1, (1,)),
                      part(k.ml_phase1, (2,)), part(k.ml_phase1, (3,)), part(k.conv_taps, (0, 1)),
                      k.softmax_pv, part(k.conv_taps, (2, 3)), part(k.ml_phase2, (0, 1)),
                      part(k.ml_phase2, (2, 3)), part(k.conv_taps, (4, 5)),
                      part(k.ml_phase3, (0, 1)), part(k.ml_phase3, (2, 3)),
                      part(k.conv_taps, (6, 7)), k.conv_out]
        begin()
        commit_staged(p, s)
        fillers.pop(0)()
        n_f = len(fillers)
        for i, unit in enumerate(units):
            unit()
            while fillers and len(fillers) > n_f - (i + 1) * n_f // len(units):
                fillers.pop(0)()

    def block_rows(q):
        return x_ref[pl.ds(_aligned(q * BLOCK, BLOCK), BLOCK), :]

    def body(i, carry):
        p = 2 * i
        maybe_set_pair_bias(p)
        half_step(p, 0, block_rows(p + 1), jnp.maximum(p - 1, 0))
        half_step(p + 1, 1, block_rows(p + 2), p)
        return carry

    lax.fori_loop(0, n_blocks // 2 - 1, body, 0)
    maybe_set_pair_bias(n_blocks - 2)
    half_step(n_blocks - 2, 0, block_rows(n_blocks - 1), n_blocks - 3)
    half_step(n_blocks - 1, 1, xn_ref[...], n_blocks - 2)
    for f in stage_c_pieces(n_blocks - 1, 1):
        f()

    extu[0:U_PAD, :] = extu[tile:tile + U_PAD, :]
    kbuf[0:ATT_LEFT, :] = kbuf[tile:tile + ATT_LEFT, :]
    vbuf[0:ATT_LEFT, :] = vbuf[tile:tile + ATT_LEFT, :]


def _const_spec(shape):
    nd = len(shape)
    return pl.BlockSpec(shape, lambda b, t: (0,) * nd)


def _layer(x, ng, w, wg, gb, aqg, akg, bias, cw, cb, mog, dww, dwb, lng, lnb, wout, *, tile):
    B, S, D = x.shape
    assert D == D_MODEL and S % tile == 0 and tile % (2 * BLOCK) == 0 and tile >= ATT_LEFT + BLOCK
    n_tiles = S // tile
    blocks_per_tile = tile // BLOCK
    consts = (ng, w, wg, gb, aqg, akg, bias, cw, cb, mog, dww, dwb, lng, lnb, wout)
    x_spec = pl.BlockSpec((None, tile, D), lambda b, t: (b, t, 0))

    def next_block(b, t):
        flat = jnp.minimum(b * n_tiles + t + 1, B * n_tiles - 1)
        return (flat // n_tiles, (flat % n_tiles) * blocks_per_tile, 0)

    return pl.pallas_call(
        functools.partial(_layer_kernel, tile=tile),
        grid=(B, n_tiles),
        in_specs=[x_spec, pl.BlockSpec((None, BLOCK, D), next_block)]
        + [_const_spec(c.shape) for c in consts],
        out_specs=x_spec,
        out_shape=jax.ShapeDtypeStruct(x.shape, x.dtype),
        scratch_shapes=(
            [pltpu.VMEM((ROW_PAD + BLOCK, D_PROJ), F32)] * 2
            + [pltpu.VMEM((16, BLOCK), F32)] * 2
            + [pltpu.VMEM((BLOCK, D_ATT), BF16)] * 6
            + [pltpu.VMEM((BLOCK, D_CONV), F32)] * 2
            + [pltpu.VMEM((BLOCK, D_MODEL), BF16)] * 2
            + [pltpu.VMEM((BLOCK, D_MODEL), BF16)]
            + [pltpu.VMEM((ATT_LEFT + tile, D_ATT), BF16)] * 2
            + [pltpu.VMEM((U_PAD + tile, D_CONV), F32),
               pltpu.VMEM((2 * BLOCK // CHUNK, ATT_HEADS * CHUNK, ATT_BAND), F32),
               pltpu.VMEM((2 * BLOCK // CHUNK, ATT_HEADS * CHUNK, ATT_BAND), F32),
               pltpu.VMEM((ML_HEADS, ML_HEAD_DIM, ML_HEAD_DIM), F32),
               pltpu.VMEM((8, ML_HEAD_DIM), F32),
               pltpu.VMEM((8, 128), F32)]),
        compiler_params=pltpu.CompilerParams(
            dimension_semantics=("arbitrary", "arbitrary"),
            vmem_limit_bytes=V7X_VMEM_LIMIT),
        name="hybrid_layer",
    )(x, x, *consts)


def _rel_bias_table(rel_bias):
    heads = rel_bias.shape[0]
    n_far = ATT_LEFT - MAX_REL + CHUNK
    n_near = ATT_BAND + CHUNK - 1 - n_far
    far = jnp.broadcast_to(rel_bias[:, 2 * MAX_REL:], (heads, n_far))
    near = rel_bias[:, 2 * MAX_REL - n_near:2 * MAX_REL][:, ::-1]
    base = jnp.concatenate([far, near], axis=1)
    rows = [base[:, CHUNK - 1 - q:CHUNK - 1 - q + ATT_BAND] for q in range(CHUNK)]
    return jnp.stack(rows, axis=1).astype(F32).reshape(heads * CHUNK, ATT_BAND)


def kernel(x, norm_g, w_in, att_q_g, att_k_g, att_rel_bias, ml_qk_conv_w, ml_qk_conv_b, ml_b_i,
           ml_b_f, ml_out_g, cv_dw_w, cv_dw_b, cv_ln_g, cv_ln_b, w_out):
    depth = w_in.shape[0]
    tile = min(SEQ_TILE, x.shape[1])
    g0 = C_CA
    g1 = g0 + 2 * ML_HEADS
    for l in range(depth):
        w = jnp.concatenate([w_in[l][:, :g0], w_in[l][:, g1:]], axis=1).astype(BF16)
        wg = jnp.zeros((16, D_MODEL), F32)
        wg = wg.at[0:ML_HEADS].set(w_in[l][:, g0:g0 + ML_HEADS].T)
        wg = wg.at[8:8 + ML_HEADS].set(w_in[l][:, g0 + ML_HEADS:g1].T)
        gb = jnp.zeros((16, BLOCK), F32)
        gb = gb.at[0:ML_HEADS].set(jnp.broadcast_to(ml_b_i[l][:, None], (ML_HEADS, BLOCK)))
        gb = gb.at[8:8 + ML_HEADS].set(jnp.broadcast_to(ml_b_f[l][:, None], (ML_HEADS, BLOCK)))
        dww = jnp.zeros((32, D_CONV), F32).at[0:CONV_WIDTH].set(cv_dw_w[l])
        x = _layer(
            x, norm_g[l][None, :], w, wg.astype(BF16), gb,
            jnp.tile(att_q_g[l], ATT_HEADS)[None, :], jnp.tile(att_k_g[l], ATT_HEADS)[None, :],
            _rel_bias_table(att_rel_bias[l]) * LOG2E,
            jnp.zeros((8, 2 * D_ML), F32).at[0:ML_CONV].set(ml_qk_conv_w[l]),
            ml_qk_conv_b[l][None, :], ml_out_g[l][None, :],
            dww, cv_dw_b[l][None, :], cv_ln_g[l][None, :], cv_ln_b[l][None, :],
            w_out[l].astype(BF16), tile=tile)
    return x
```

```python
import functools

import jax
import jax.numpy as jnp
from jax import lax
from jax.experimental import pallas as pl
from jax.experimental.pallas import tpu as pltpu

F32 = jnp.float32
BF16 = jnp.bfloat16

CHUNK = 64
EPS = 1e-6
NEG = -1e30
LOG2E = 1.4426950408889634

D_MODEL = 1024
ATT_HEADS = 4
ATT_HEAD_DIM = 64
D_ATT = ATT_HEADS * ATT_HEAD_DIM
ATT_LEFT = 8 * CHUNK
ATT_BAND = ATT_LEFT + CHUNK
MAX_REL = 128
ML_HEADS = 4
ML_HEAD_DIM = 128
D_ML = ML_HEADS * ML_HEAD_DIM
ML_CONV = 4
D_CONV = 256
CONV_WIDTH = 31

C_AQ, C_AK, C_AV, C_AZ = 0, 256, 512, 768
C_MQ, C_MK, C_MV, C_MO, C_MZ = 1024, 1536, 2048, 2560, 3072
C_CA, C_CB, C_CZ = 3584, 3840, 4096
D_PROJ = 4352
PIECE = 256
X_ATT, X_ML, X_CV = 0, 256, 768

BLOCK = 4 * CHUNK
CHUNKS_PER_BLOCK = BLOCK // CHUNK
ROW_PAD = 8
U_PAD = 32
SEQ_TILE = 512
V7X_VMEM_LIMIT = 58 * 1024 * 1024


def _sigmoid(x):
    return 0.5 * jnp.tanh(0.5 * x) + 0.5


def _silu(x):
    hx = 0.5 * x
    return hx * jnp.tanh(hx) + hx


def _log_sigmoid(x):
    return jnp.minimum(x, 0.0) - jnp.log1p(jnp.exp(-jnp.abs(x)))


def _split_dot(x, w_bf16):
    hi = x.astype(BF16)
    lo = (x - hi.astype(F32)).astype(BF16)
    return (jnp.dot(hi, w_bf16, preferred_element_type=F32)
            + jnp.dot(lo, w_bf16, preferred_element_type=F32))


def _aligned(v, m):
    return v if isinstance(v, int) else pl.multiple_of(v, m)


def _shift_rows(x, k):
    if k % x.shape[0] == 0:
        return x
    return pltpu.roll(x, x.shape[0] - k, axis=0)


def _layer_kernel(x_ref, xn_ref, ng_ref, w_ref, wg_ref, gb_ref, aqg_ref, akg_ref, bias_ref,
                  cw_ref, cb_ref, mog_ref, dww_ref, dwb_ref, lng_ref, lnb_ref, wout_ref,
                  o_ref,
                  proj0, proj1, gsc0, gsc1, qn0, qn1, kst0, kst1, vst0, vst1, ust0, ust1,
                  mixed0, mixed1, hbuf, kbuf, vbuf, extu, bias_c, scb, ct, nst, mst, *, tile):
    b = pl.program_id(0)
    t = pl.program_id(1)
    n_blocks = tile // BLOCK
    proj = (proj0, proj1)
    gsc = (gsc0, gsc1)
    qn = (qn0, qn1)
    kst = (kst0, kst1)
    vst = (vst0, vst1)
    ust = (ust0, ust1)
    mixed = (mixed0, mixed1)

    r_i = lax.broadcasted_iota(jnp.int32, (D_ATT, D_ATT), 0) // ATT_HEAD_DIM
    c_i = lax.broadcasted_iota(jnp.int32, (D_ATT, D_ATT), 1) // ATT_HEAD_DIM
    same_head = r_i == c_i
    head_mean = jnp.where(same_head, 1.0 / ATT_HEAD_DIM, 0.0).astype(BF16)
    head_one = jnp.where(same_head, 1.0, 0.0).astype(BF16)
    key_col = lax.broadcasted_iota(jnp.int32, (1, ATT_BAND), 1)
    lane_att = lax.broadcasted_iota(jnp.int32, (CHUNK, D_ATT), 1)
    row64 = lax.broadcasted_iota(jnp.int32, (CHUNK, CHUNK), 0)
    col64 = lax.broadcasted_iota(jnp.int32, (CHUNK, CHUNK), 1)
    tril = row64 >= col64
    eye = row64 == col64
    lane_in_chunk = lax.broadcasted_iota(jnp.int32, (8, 128), 1) % CHUNK

    def stage_a_pieces(x, s):
        def begin():
            ms = jnp.mean(x * x, axis=-1, keepdims=True)
            hbuf[...] = (x * lax.rsqrt(ms + EPS) * ng_ref[...]).astype(BF16)
            proj[s][0:ROW_PAD, C_MQ:C_MV] = proj[1 - s][BLOCK:BLOCK + ROW_PAD, C_MQ:C_MV]

        def project(c0):
            c1 = min(c0 + PIECE, D_PROJ)
            proj[s][ROW_PAD:ROW_PAD + BLOCK, c0:c1] = jnp.dot(
                hbuf[...], w_ref[:, c0:c1], preferred_element_type=F32)

        def gates():
            g16 = lax.dot_general(wg_ref[...], hbuf[...], (((1,), (1,)), ((), ())),
                                  preferred_element_type=F32) + gb_ref[...]
            for c0 in range(0, BLOCK, 128):
                lf8 = _log_sigmoid(g16[8:16, c0:c0 + 128])
                b8 = lf8
                for sh in (1, 2, 4, 8, 16, 32):
                    b8 = b8 + jnp.where(lane_in_chunk >= sh, pltpu.roll(b8, sh, axis=1), 0.0)
                gsc[s][0:8, c0:c0 + 128] = g16[0:8, c0:c0 + 128] - b8
                gsc[s][8:16, c0:c0 + 128] = lf8

        def qk_norm():
            rows = slice(ROW_PAD, ROW_PAD + BLOCK)
            q = proj[s][rows, C_AQ:C_AQ + D_ATT]
            q_ms = _split_dot(q * q, head_mean)
            qn[s][...] = (q * lax.rsqrt(q_ms + EPS)
                          * (aqg_ref[...] * (ATT_HEAD_DIM ** -0.5 * LOG2E))).astype(BF16)
            k = proj[s][rows, C_AK:C_AK + D_ATT]
            k_ms = _split_dot(k * k, head_mean)
            kst[s][...] = (k * lax.rsqrt(k_ms + EPS) * akg_ref[...]).astype(BF16)

        def stage_v():
            vst[s][...] = proj[s][ROW_PAD:ROW_PAD + BLOCK, C_AV:C_AV + D_ATT].astype(BF16)

        def stage_u():
            rows = slice(ROW_PAD, ROW_PAD + BLOCK)
            ust[s][...] = (proj[s][rows, C_CA:C_CA + D_CONV]
                           * _sigmoid(proj[s][rows, C_CB:C_CB + D_CONV]))

        def piece(c0, *extra):
            def run():
                project(c0)
                for f in extra:
                    f()
            return run

        starts = list(range(0, D_PROJ, PIECE))
        extras = {min(c0 for c0 in starts if c0 >= C_AV): (gates,),
                  min(c0 for c0 in starts if c0 >= C_AZ): (qk_norm, stage_v),
                  starts[-1]: (stage_u,)}
        return begin, [piece(c0, *extras.get(c0, ())) for c0 in starts]

    class Chunk:
        def __init__(self, p, s, j):
            self.s = s
            self.rr = j * CHUNK
            self.r0 = _aligned(p * BLOCK + self.rr, CHUNK)
            self.rows = slice(ROW_PAD + self.rr, ROW_PAD + self.rr + CHUNK)
            self.bias = bias_c.at[CHUNKS_PER_BLOCK * s + j]
            self.scr = scb.at[CHUNKS_PER_BLOCK * s + j]
            self.heads = {}
            self.conv_acc = None

        def scores(self):
            s, rr = self.s, self.rr
            qc = qn[s][rr:rr + CHUNK, :]
            q_bd = jnp.concatenate([qc] * ATT_HEADS, axis=0) * head_one
            k_band = kbuf[pl.ds(self.r0, ATT_BAND), :]
            self.scr[...] = lax.dot_general(q_bd, k_band, (((1,), (1,)), ((), ())),
                                            preferred_element_type=F32) + self.bias[...]

        def softmax_pv(self):
            s, rr = self.s, self.rr
            sc = self.scr[...]
            pe = jnp.exp2(sc - jnp.max(sc, axis=-1, keepdims=True))
            inv_l = 1.0 / jnp.sum(pe, axis=-1, keepdims=True)
            v_band = vbuf[pl.ds(self.r0, ATT_BAND), :]
            o = jnp.dot(pe.astype(BF16), v_band, preferred_element_type=F32) * inv_l
            att = jnp.where(
                lane_att < ATT_HEAD_DIM, o[0:CHUNK],
                jnp.where(lane_att < 2 * ATT_HEAD_DIM, o[CHUNK:2 * CHUNK],
                          jnp.where(lane_att < 3 * ATT_HEAD_DIM, o[2 * CHUNK:3 * CHUNK],
                                    o[3 * CHUNK:4 * CHUNK])))
            att = att * _silu(proj[s][self.rows, C_AZ:C_AZ + D_ATT])
            mixed[s][rr:rr + CHUNK, X_ATT:X_ATT + D_ATT] = att.astype(BF16)

        def ml_phase1(self, heads):
            s, rr = self.s, self.rr
            for h in heads:
                hc = h * ML_HEAD_DIM
                wq = proj[s][rr:rr + ROW_PAD + CHUNK, C_MQ + hc:C_MQ + hc + ML_HEAD_DIM]
                wk = proj[s][rr:rr + ROW_PAD + CHUNK, C_MK + hc:C_MK + hc + ML_HEAD_DIM]
                q_acc = cb_ref[:, hc:hc + ML_HEAD_DIM]
                k_acc = cb_ref[:, D_ML + hc:D_ML + hc + ML_HEAD_DIM]
                for jj in range(ML_CONV):
                    off = ROW_PAD - (ML_CONV - 1) + jj
                    q_acc = q_acc + (cw_ref[jj:jj + 1, hc:hc + ML_HEAD_DIM]
                                     * _shift_rows(wq, off)[0:CHUNK])
                    k_acc = k_acc + (cw_ref[jj:jj + 1, D_ML + hc:D_ML + hc + ML_HEAD_DIM]
                                     * _shift_rows(wk, off)[0:CHUNK])
                q_h = _silu(q_acc)
                k_h = _silu(k_acc) * ML_HEAD_DIM ** -0.5
                q_b = q_h.astype(BF16)
                k_b = k_h.astype(BF16)
                qk = lax.dot_general(q_b, k_b, (((1,), (1,)), ((), ())),
                                     preferred_element_type=F32)
                self.heads[h] = dict(hc=hc, q_h=q_h, k_h=k_h, q_b=q_b, k_b=k_b, qk=qk)

        def ml_phase2(self, heads):
            s, rr = self.s, self.rr
            a8 = gsc[s][0:8, :]
            lf8 = gsc[s][8:16, :]
            for h in heads:
                hd = self.heads[h]
                lf_row = lf8[h:h + 1, rr:rr + CHUNK]
                a_row = a8[h:h + 1, rr:rr + CHUNK]
                m_prev = mst[h:h + 1, 0:1]
                b_col = jnp.sum(jnp.where(tril, lf_row, 0.0), axis=-1, keepdims=True)
                a_col = jnp.sum(jnp.where(eye, a_row, 0.0), axis=-1, keepdims=True)
                g_col = jnp.maximum(
                    jnp.max(jnp.where(tril, a_row, NEG), axis=-1, keepdims=True), m_prev)
                d_mat = jnp.where(tril, jnp.exp(a_row - g_col), 0.0)
                mst[h:h + 1, :] = jnp.broadcast_to(
                    b_col[CHUNK - 1:CHUNK, :] + g_col[CHUNK - 1:CHUNK, :], (1, 128))
                v_h = proj[s][self.rows, C_MV + hd["hc"]:C_MV + hd["hc"] + ML_HEAD_DIM]
                s_m = hd["qk"] * d_mat
                sv = jnp.dot(s_m.astype(BF16), v_h.astype(BF16), preferred_element_type=F32)
                hd.update(m_prev=m_prev, b_col=b_col, a_col=a_col, g_col=g_col, v_h=v_h,
                          s_m=s_m, sv=sv)

        def ml_phase3(self, heads):
            s, rr = self.s, self.rr
            for h in heads:
                hd = self.heads[h]
                hc = hd["hc"]
                m_prev, g_col, b_col = hd["m_prev"], hd["g_col"], hd["b_col"]
                n_row = nst[h:h + 1, :]
                inter_w = jnp.exp(m_prev - g_col)
                q_c = jnp.dot(hd["q_b"], ct[h].astype(BF16), preferred_element_type=F32)
                num = hd["sv"] + inter_w * q_c
                den = (jnp.sum(hd["s_m"], axis=-1, keepdims=True)
                       + inter_w * jnp.sum(hd["q_h"] * n_row, axis=-1, keepdims=True))
                h_val = num * (1.0 / jnp.maximum(jnp.abs(den), jnp.exp(-(b_col + g_col))))

                g_last = g_col[CHUNK - 1:CHUNK, :]
                decay = jnp.exp(m_prev - g_last)
                w_col = jnp.exp(hd["a_col"] - g_last)
                ct[h] = decay * ct[h] + lax.dot_general(
                    hd["k_b"], (w_col * hd["v_h"]).astype(BF16), (((0,), (0,)), ((), ())),
                    preferred_element_type=F32)
                nst[h:h + 1, :] = decay * n_row + jnp.sum(w_col * hd["k_h"], axis=0, keepdims=True)

                hm = _sigmoid(proj[s][self.rows, C_MO + hc:C_MO + hc + ML_HEAD_DIM]) * h_val
                hm = hm * lax.rsqrt(jnp.mean(hm * hm, axis=-1, keepdims=True) + EPS)
                hm = hm * mog_ref[:, hc:hc + ML_HEAD_DIM]
                ml = hm * _silu(proj[s][self.rows, C_MZ + hc:C_MZ + hc + ML_HEAD_DIM])
                mixed[s][rr:rr + CHUNK, X_ML + hc:X_ML + hc + ML_HEAD_DIM] = ml.astype(BF16)

        def conv_taps(self, residues):
            acc = self.conv_acc
            if acc is None:
                acc = jnp.broadcast_to(dwb_ref[...], (CHUNK, D_CONV))
            uwin = extu[pl.ds(self.r0, CHUNK + U_PAD), :]
            first = U_PAD - (CONV_WIDTH - 1)
            for res in residues:
                shifted = _shift_rows(uwin, res)
                for o_ in range(first, U_PAD + 1):
                    if o_ % 8 == res:
                        tap = o_ - first
                        acc = acc + dww_ref[tap:tap + 1, :] * shifted[o_ - res:o_ - res + CHUNK]
            self.conv_acc = acc

        def conv_out(self):
            s, rr = self.s, self.rr
            acc = self.conv_acc
            mu = jnp.mean(acc, axis=-1, keepdims=True)
            xc = acc - mu
            y = xc * lax.rsqrt(jnp.mean(xc * xc, axis=-1, keepdims=True) + EPS)
            y = _silu(y * lng_ref[...] + lnb_ref[...])
            cv = y * _silu(proj[s][self.rows, C_CZ:C_CZ + D_CONV])
            mixed[s][rr:rr + CHUNK, X_CV:X_CV + D_CONV] = cv.astype(BF16)

    def commit_staged(p, s):
        kbuf[pl.ds(_aligned(ATT_LEFT + p * BLOCK, BLOCK), BLOCK), :] = kst[s][...]
        vbuf[pl.ds(_aligned(ATT_LEFT + p * BLOCK, BLOCK), BLOCK), :] = vst[s][...]
        extu[pl.ds(_aligned(U_PAD + p * BLOCK, U_PAD), BLOCK), :] = ust[s][...]

    def stage_c_pieces(p, s):
        rows = pl.ds(_aligned(p * BLOCK, BLOCK), BLOCK)

        def piece(c0):
            def run():
                o_ref[rows, c0:c0 + PIECE] = x_ref[rows, c0:c0 + PIECE] + jnp.dot(
                    mixed[s][...], wout_ref[:, c0:c0 + PIECE], preferred_element_type=F32)
            return run

        return [piece(c0) for c0 in range(0, D_MODEL, PIECE)]

    def maybe_set_pair_bias(p):
        @pl.when(t * tile + p * BLOCK <= ATT_LEFT)
        def _():
            for j in range(2 * BLOCK // CHUNK):
                first_valid = ATT_LEFT - (t * tile + p * BLOCK + j * CHUNK)
                bias_c[j] = bias_ref[...] + jnp.where(key_col >= first_valid, 0.0, NEG)

    @pl.when(jnp.logical_and(b == 0, t == 0))
    def _first_step():
        proj1[BLOCK:BLOCK + ROW_PAD, C_MQ:C_MV] = jnp.zeros((ROW_PAD, C_MV - C_MQ), F32)
        mixed1[...] = jnp.zeros(mixed1.shape, BF16)
        begin, pieces = stage_a_pieces(x_ref[0:BLOCK, :], 0)
        begin()
        for f in pieces:
            f()

    @pl.when(t == 0)
    def _reset():
        proj0[0:ROW_PAD, C_MQ:C_MV] = jnp.zeros((ROW_PAD, C_MV - C_MQ), F32)
        kbuf[0:ATT_LEFT, :] = jnp.zeros((ATT_LEFT, D_ATT), BF16)
        vbuf[0:ATT_LEFT, :] = jnp.zeros((ATT_LEFT, D_ATT), BF16)
        extu[0:U_PAD, :] = jnp.zeros((U_PAD, D_CONV), F32)
        ct[...] = jnp.zeros(ct.shape, F32)
        nst[...] = jnp.zeros(nst.shape, F32)
        mst[...] = jnp.zeros(mst.shape, F32)

    def half_step(p, s, x_next, p_out):
        begin, a = stage_a_pieces(x_next, 1 - s)
        fillers = stage_c_pieces(p_out, 1 - s) + a
        units = []
        for k in [Chunk(p, s, j) for j in range(CHUNKS_PER_BLOCK)]:
            part = functools.partial
            units += [k.scores, part(k.ml_phase1, (0,)), part(k.ml_phase1, (1,)),
                      part(k.ml_phase1, (2,)), part(k.ml_phase1, (3,)), part(k.conv_taps, (0, 1)),
                      k.softmax_pv, part(k.conv_taps, (2, 3)), part(k.ml_phase2, (0, 1)),
                      part(k.ml_phase2, (2, 3)), part(k.conv_taps, (4, 5)),
                      part(k.ml_phase3, (0, 1)), part(k.ml_phase3, (2, 3)),
                      part(k.conv_taps, (6, 7)), k.conv_out]
        begin()
        commit_staged(p, s)
        fillers.pop(0)()
        n_f = len(fillers)
        for i, unit in enumerate(units):
            unit()
            while fillers and len(fillers) > n_f - (i + 1) * n_f // len(units):
                fillers.pop(0)()

    def block_rows(q):
        return x_ref[pl.ds(_aligned(q * BLOCK, BLOCK), BLOCK), :]

    def body(i, carry):
        p = 2 * i
        maybe_set_pair_bias(p)
        half_step(p, 0, block_rows(p + 1), jnp.maximum(p - 1, 0))
        half_step(p + 1, 1, block_rows(p + 2), p)
        return carry

    if n_blocks > 2:
        lax.fori_loop(0, n_blocks // 2 - 1, body, 0)
    maybe_set_pair_bias(n_blocks - 2)
    half_step(n_blocks - 2, 0, block_rows(n_blocks - 1), max(n_blocks - 3, 0))
    half_step(n_blocks - 1, 1, xn_ref[...], n_blocks - 2)
    for f in stage_c_pieces(n_blocks - 1, 1):
        f()

    extu[0:U_PAD, :] = extu[tile:tile + U_PAD, :]
    kbuf[0:ATT_LEFT, :] = kbuf[tile:tile + ATT_LEFT, :]
    vbuf[0:ATT_LEFT, :] = vbuf[tile:tile + ATT_LEFT, :]


def _const_spec(shape):
    nd = len(shape)
    return pl.BlockSpec(shape, lambda b, t: (0,) * nd)


def _layer(x, ng, w, wg, gb, aqg, akg, bias, cw, cb, mog, dww, dwb, lng, lnb, wout, *, tile):
    B, S, D = x.shape
    assert D == D_MODEL and S % tile == 0 and tile % (2 * BLOCK) == 0 and tile >= ATT_LEFT
    n_tiles = S // tile
    blocks_per_tile = tile // BLOCK
    consts = (ng, w, wg, gb, aqg, akg, bias, cw, cb, mog, dww, dwb, lng, lnb, wout)
    x_spec = pl.BlockSpec((None, tile, D), lambda b, t: (b, t, 0))

    def next_block(b, t):
        flat = jnp.minimum(b * n_tiles + t + 1, B * n_tiles - 1)
        return (flat // n_tiles, (flat % n_tiles) * blocks_per_tile, 0)

    return pl.pallas_call(
        functools.partial(_layer_kernel, tile=tile),
        grid=(B, n_tiles),
        in_specs=[x_spec, pl.BlockSpec((None, BLOCK, D), next_block)]
        + [_const_spec(c.shape) for c in consts],
        out_specs=x_spec,
        out_shape=jax.ShapeDtypeStruct(x.shape, x.dtype),
        scratch_shapes=(
            [pltpu.VMEM((ROW_PAD + BLOCK, D_PROJ), F32)] * 2
            + [pltpu.VMEM((16, BLOCK), F32)] * 2
            + [pltpu.VMEM((BLOCK, D_ATT), BF16)] * 6
            + [pltpu.VMEM((BLOCK, D_CONV), F32)] * 2
            + [pltpu.VMEM((BLOCK, D_MODEL), BF16)] * 2
            + [pltpu.VMEM((BLOCK, D_MODEL), BF16)]
            + [pltpu.VMEM((ATT_LEFT + tile, D_ATT), BF16)] * 2
            + [pltpu.VMEM((U_PAD + tile, D_CONV), F32),
               pltpu.VMEM((2 * BLOCK // CHUNK, ATT_HEADS * CHUNK, ATT_BAND), F32),
               pltpu.VMEM((2 * BLOCK // CHUNK, ATT_HEADS * CHUNK, ATT_BAND), F32),
               pltpu.VMEM((ML_HEADS, ML_HEAD_DIM, ML_HEAD_DIM), F32),
               pltpu.VMEM((8, ML_HEAD_DIM), F32),
               pltpu.VMEM((8, 128), F32)]),
        compiler_params=pltpu.CompilerParams(
            dimension_semantics=("arbitrary", "arbitrary"),
            vmem_limit_bytes=V7X_VMEM_LIMIT),
        name="hybrid_layer",
    )(x, x, *consts)


def _rel_bias_table(rel_bias):
    heads = rel_bias.shape[0]
    n_far = ATT_LEFT - MAX_REL + CHUNK
    n_near = ATT_BAND + CHUNK - 1 - n_far
    far = jnp.broadcast_to(rel_bias[:, 2 * MAX_REL:], (heads, n_far))
    near = rel_bias[:, 2 * MAX_REL - n_near:2 * MAX_REL][:, ::-1]
    base = jnp.concatenate([far, near], axis=1)
    rows = [base[:, CHUNK - 1 - q:CHUNK - 1 - q + ATT_BAND] for q in range(CHUNK)]
    return jnp.stack(rows, axis=1).astype(F32).reshape(heads * CHUNK, ATT_BAND)


def kernel(x, norm_g, w_in, att_q_g, att_k_g, att_rel_bias, ml_qk_conv_w, ml_qk_conv_b, ml_b_i,
           ml_b_f, ml_out_g, cv_dw_w, cv_dw_b, cv_ln_g, cv_ln_b, w_out):
    depth = w_in.shape[0]
    tile = min(SEQ_TILE, x.shape[1])
    g0 = C_CA
    g1 = g0 + 2 * ML_HEADS
    for l in range(depth):
        w = jnp.concatenate([w_in[l][:, :g0], w_in[l][:, g1:]], axis=1).astype(BF16)
        wg = jnp.zeros((16, D_MODEL), F32)
        wg = wg.at[0:ML_HEADS].set(w_in[l][:, g0:g0 + ML_HEADS].T)
        wg = wg.at[8:8 + ML_HEADS].set(w_in[l][:, g0 + ML_HEADS:g1].T)
        gb = jnp.zeros((16, BLOCK), F32)
        gb = gb.at[0:ML_HEADS].set(jnp.broadcast_to(ml_b_i[l][:, None], (ML_HEADS, BLOCK)))
        gb = gb.at[8:8 + ML_HEADS].set(jnp.broadcast_to(ml_b_f[l][:, None], (ML_HEADS, BLOCK)))
        dww = jnp.zeros((32, D_CONV), F32).at[0:CONV_WIDTH].set(cv_dw_w[l])
        x = _layer(
            x, norm_g[l][None, :], w, wg.astype(BF16), gb,
            jnp.tile(att_q_g[l], ATT_HEADS)[None, :], jnp.tile(att_k_g[l], ATT_HEADS)[None, :],
            _rel_bias_table(att_rel_bias[l]) * LOG2E,
            jnp.zeros((8, 2 * D_ML), F32).at[0:ML_CONV].set(ml_qk_conv_w[l]),
            ml_qk_conv_b[l][None, :], ml_out_g[l][None, :],
            dww, cv_dw_b[l][None, :], cv_ln_g[l][None, :], cv_ln_b[l][None, :],
            w_out[l].astype(BF16), tile=tile)
    return x
```

```python
import functools

import jax
import jax.numpy as jnp
from jax import lax
from jax.experimental import pallas as pl
from jax.experimental.pallas import tpu as pltpu

F32 = jnp.float32
BF16 = jnp.bfloat16

CHUNK = 64
EPS = 1e-6
NEG = -1e30
LOG2E = 1.4426950408889634

D_MODEL = 1024
ATT_HEADS = 4
ATT_HEAD_DIM = 64
D_ATT = ATT_HEADS * ATT_HEAD_DIM
ATT_LEFT = 8 * CHUNK
ATT_BAND = ATT_LEFT + CHUNK
MAX_REL = 128
ML_HEADS = 4
ML_HEAD_DIM = 128
D_ML = ML_HEADS * ML_HEAD_DIM
ML_CONV = 4
D_CONV = 256
CONV_WIDTH = 31

C_AQ, C_AK, C_AV, C_AZ = 0, 256, 512, 768
C_MQ, C_MK, C_MV, C_MO, C_MZ = 1024, 1536, 2048, 2560, 3072
C_CA, C_CB, C_CZ = 3584, 3840, 4096
D_PROJ = 4352
PIECE = 256
X_ATT, X_ML, X_CV = 0, 256, 768

BLOCK = 4 * CHUNK
CHUNKS_PER_BLOCK = BLOCK // CHUNK
ROW_PAD = 8
U_PAD = 32
SEQ_TILE = 512
V7X_VMEM_LIMIT = 58 * 1024 * 1024


def _sigmoid(x):
    return 0.5 * jnp.tanh(0.5 * x) + 0.5


def _silu(x):
    hx = 0.5 * x
    return hx * jnp.tanh(hx) + hx


def _log_sigmoid(x):
    return jnp.minimum(x, 0.0) - jnp.log1p(jnp.exp(-jnp.abs(x)))


def _split_dot(x, w_bf16):
    hi = x.astype(BF16)
    lo = (x - hi.astype(F32)).astype(BF16)
    return (jnp.dot(hi, w_bf16, preferred_element_type=F32)
            + jnp.dot(lo, w_bf16, preferred_element_type=F32))


def _aligned(v, m):
    return v if isinstance(v, int) else pl.multiple_of(v, m)


def _shift_rows(x, k):
    if k % x.shape[0] == 0:
        return x
    return pltpu.roll(x, x.shape[0] - k, axis=0)


def _layer_kernel(x_ref, xn_ref, ng_ref, w_ref, wg_ref, gb_ref, aqg_ref, akg_ref, bias_ref,
                  cw_ref, cb_ref, mog_ref, dww_ref, dwb_ref, lng_ref, lnb_ref, wout_ref,
                  o_ref,
                  proj0, proj1, gsc0, gsc1, qn0, qn1, kst0, kst1, vst0, vst1, ust0, ust1,
                  mixed0, mixed1, hbuf, kbuf, vbuf, extu, bias_c, scb, ct, nst, mst, *, tile):
    b = pl.program_id(0)
    t = pl.program_id(1)
    n_blocks = tile // BLOCK
    proj = (proj0, proj1)
    gsc = (gsc0, gsc1)
    qn = (qn0, qn1)
    kst = (kst0, kst1)
    vst = (vst0, vst1)
    ust = (ust0, ust1)
    mixed = (mixed0, mixed1)

    r_i = lax.broadcasted_iota(jnp.int32, (D_ATT, D_ATT), 0) // ATT_HEAD_DIM
    c_i = lax.broadcasted_iota(jnp.int32, (D_ATT, D_ATT), 1) // ATT_HEAD_DIM
    same_head = r_i == c_i
    head_mean = jnp.where(same_head, 1.0 / ATT_HEAD_DIM, 0.0).astype(BF16)
    head_one = jnp.where(same_head, 1.0, 0.0).astype(BF16)
    key_col = lax.broadcasted_iota(jnp.int32, (1, ATT_BAND), 1)
    lane_att = lax.broadcasted_iota(jnp.int32, (CHUNK, D_ATT), 1)
    row64 = lax.broadcasted_iota(jnp.int32, (CHUNK, CHUNK), 0)
    col64 = lax.broadcasted_iota(jnp.int32, (CHUNK, CHUNK), 1)
    tril = row64 >= col64
    eye = row64 == col64
    lane_in_chunk = lax.broadcasted_iota(jnp.int32, (8, 128), 1) % CHUNK

    def stage_a_pieces(x, s):
        def begin():
            ms = jnp.mean(x * x, axis=-1, keepdims=True)
            hbuf[...] = (x * lax.rsqrt(ms + EPS) * ng_ref[...]).astype(BF16)
            proj[s][0:ROW_PAD, C_MQ:C_MV] = proj[1 - s][BLOCK:BLOCK + ROW_PAD, C_MQ:C_MV]

        def project(c0):
            c1 = min(c0 + PIECE, D_PROJ)
            proj[s][ROW_PAD:ROW_PAD + BLOCK, c0:c1] = jnp.dot(
                hbuf[...], w_ref[:, c0:c1], preferred_element_type=F32)

        def gates():
            g16 = lax.dot_general(wg_ref[...], hbuf[...], (((1,), (1,)), ((), ())),
                                  preferred_element_type=F32) + gb_ref[...]
            for c0 in range(0, BLOCK, 128):
                lf8 = _log_sigmoid(g16[8:16, c0:c0 + 128])
                b8 = lf8
                for sh in (1, 2, 4, 8, 16, 32):
                    b8 = b8 + jnp.where(lane_in_chunk >= sh, pltpu.roll(b8, sh, axis=1), 0.0)
                gsc[s][0:8, c0:c0 + 128] = g16[0:8, c0:c0 + 128] - b8
                gsc[s][8:16, c0:c0 + 128] = lf8

        def qk_norm():
            rows = slice(ROW_PAD, ROW_PAD + BLOCK)
            q = proj[s][rows, C_AQ:C_AQ + D_ATT]
            q_ms = _split_dot(q * q, head_mean)
            qn[s][...] = (q * lax.rsqrt(q_ms + EPS)
                          * (aqg_ref[...] * (ATT_HEAD_DIM ** -0.5 * LOG2E))).astype(BF16)
            k = proj[s][rows, C_AK:C_AK + D_ATT]
            k_ms = _split_dot(k * k, head_mean)
            kst[s][...] = (k * lax.rsqrt(k_ms + EPS) * akg_ref[...]).astype(BF16)

        def stage_v():
            vst[s][...] = proj[s][ROW_PAD:ROW_PAD + BLOCK, C_AV:C_AV + D_ATT].astype(BF16)

        def stage_u():
            rows = slice(ROW_PAD, ROW_PAD + BLOCK)
            ust[s][...] = (proj[s][rows, C_CA:C_CA + D_CONV]
                           * _sigmoid(proj[s][rows, C_CB:C_CB + D_CONV]))

        def piece(c0, *extra):
            def run():
                project(c0)
                for f in extra:
                    f()
            return run

        starts = list(range(0, D_PROJ, PIECE))
        extras = {min(c0 for c0 in starts if c0 >= C_AV): (gates,),
                  min(c0 for c0 in starts if c0 >= C_AZ): (qk_norm, stage_v),
                  starts[-1]: (stage_u,)}
        return begin, [piece(c0, *extras.get(c0, ())) for c0 in starts]

    class Chunk:
        def __init__(self, p, s, j):
            self.s = s
            self.rr = j * CHUNK
            self.r0 = _aligned(p * BLOCK + self.rr, CHUNK)
            self.rows = slice(ROW_PAD + self.rr, ROW_PAD + self.rr + CHUNK)
            self.bias = bias_c.at[CHUNKS_PER_BLOCK * s + j]
            self.scr = scb.at[CHUNKS_PER_BLOCK * s + j]
            self.heads = {}
            self.conv_acc = None

        def scores(self):
            s, rr = self.s, self.rr
            qc = qn[s][rr:rr + CHUNK, :]
            q_bd = jnp.concatenate([qc] * ATT_HEADS, axis=0) * head_one
            k_band = kbuf[pl.ds(self.r0, ATT_BAND), :]
            self.scr[...] = lax.dot_general(q_bd, k_band, (((1,), (1,)), ((), ())),
                                            preferred_element_type=F32) + self.bias[...]

        def softmax_pv(self):
            s, rr = self.s, self.rr
            sc = self.scr[...]
            pe = jnp.exp2(sc - jnp.max(sc, axis=-1, keepdims=True))
            inv_l = 1.0 / jnp.sum(pe, axis=-1, keepdims=True)
            v_band = vbuf[pl.ds(self.r0, ATT_BAND), :]
            o = jnp.dot(pe.astype(BF16), v_band, preferred_element_type=F32) * inv_l
            att = jnp.where(
                lane_att < ATT_HEAD_DIM, o[0:CHUNK],
                jnp.where(lane_att < 2 * ATT_HEAD_DIM, o[CHUNK:2 * CHUNK],
                          jnp.where(lane_att < 3 * ATT_HEAD_DIM, o[2 * CHUNK:3 * CHUNK],
                                    o[3 * CHUNK:4 * CHUNK])))
            att = att * _silu(proj[s][self.rows, C_AZ:C_AZ + D_ATT])
            mixed[s][rr:rr + CHUNK, X_ATT:X_ATT + D_ATT] = att.astype(BF16)

        def ml_phase1(self, heads):
            s, rr = self.s, self.rr
            for h in heads:
                hc = h * ML_HEAD_DIM
                wq = proj[s][rr:rr + ROW_PAD + CHUNK, C_MQ + hc:C_MQ + hc + ML_HEAD_DIM]
                wk = proj[s][rr:rr + ROW_PAD + CHUNK, C_MK + hc:C_MK + hc + ML_HEAD_DIM]
                q_acc = cb_ref[:, hc:hc + ML_HEAD_DIM]
                k_acc = cb_ref[:, D_ML + hc:D_ML + hc + ML_HEAD_DIM]
                for jj in range(ML_CONV):
                    off = ROW_PAD - (ML_CONV - 1) + jj
                    q_acc = q_acc + (cw_ref[jj:jj + 1, hc:hc + ML_HEAD_DIM]
                                     * _shift_rows(wq, off)[0:CHUNK])
                    k_acc = k_acc + (cw_ref[jj:jj + 1, D_ML + hc:D_ML + hc + ML_HEAD_DIM]
                                     * _shift_rows(wk, off)[0:CHUNK])
                q_h = _silu(q_acc)
                k_h = _silu(k_acc) * ML_HEAD_DIM ** -0.5
                q_b = q_h.astype(BF16)
                k_b = k_h.astype(BF16)
                qk = lax.dot_general(q_b, k_b, (((1,), (1,)), ((), ())),
                                     preferred_element_type=F32)
                self.heads[h] = dict(hc=hc, q_h=q_h, k_h=k_h, q_b=q_b, k_b=k_b, qk=qk)

        def ml_phase2(self, heads):
            s, rr = self.s, self.rr
            a8 = gsc[s][0:8, :]
            lf8 = gsc[s][8:16, :]
            for h in heads:
                hd = self.heads[h]
                lf_row = lf8[h:h + 1, rr:rr + CHUNK]
                a_row = a8[h:h + 1, rr:rr + CHUNK]
                m_prev = mst[h:h + 1, 0:1]
                b_col = jnp.sum(jnp.where(tril, lf_row, 0.0), axis=-1, keepdims=True)
                a_col = jnp.sum(jnp.where(eye, a_row, 0.0), axis=-1, keepdims=True)
                g_col = jnp.maximum(
                    jnp.max(jnp.where(tril, a_row, NEG), axis=-1, keepdims=True), m_prev)
                d_mat = jnp.where(tril, jnp.exp(a_row - g_col), 0.0)
                mst[h:h + 1, :] = jnp.broadcast_to(
                    b_col[CHUNK - 1:CHUNK, :] + g_col[CHUNK - 1:CHUNK, :], (1, 128))
                v_h = proj[s][self.rows, C_MV + hd["hc"]:C_MV + hd["hc"] + ML_HEAD_DIM]
                s_m = hd["qk"] * d_mat
                sv = jnp.dot(s_m.astype(BF16), v_h.astype(BF16), preferred_element_type=F32)
                hd.update(m_prev=m_prev, b_col=b_col, a_col=a_col, g_col=g_col, v_h=v_h,
                          s_m=s_m, sv=sv)

        def ml_phase3(self, heads):
            s, rr = self.s, self.rr
            for h in heads:
                hd = self.heads[h]
                hc = hd["hc"]
                m_prev, g_col, b_col = hd["m_prev"], hd["g_col"], hd["b_col"]
                n_row = nst[h:h + 1, :]
                inter_w = jnp.exp(m_prev - g_col)
                q_c = jnp.dot(hd["q_b"], ct[h].astype(BF16), preferred_element_type=F32)
                num = hd["sv"] + inter_w * q_c
                den = (jnp.sum(hd["s_m"], axis=-1, keepdims=True)
                       + inter_w * jnp.sum(hd["q_h"] * n_row, axis=-1, keepdims=True))
                h_val = num * (1.0 / jnp.maximum(jnp.abs(den), jnp.exp(-(b_col + g_col))))

                g_last = g_col[CHUNK - 1:CHUNK, :]
                decay = jnp.exp(m_prev - g_last)
                w_col = jnp.exp(hd["a_col"] - g_last)
                ct[h] = decay * ct[h] + lax.dot_general(
                    hd["k_b"], (w_col * hd["v_h"]).astype(BF16), (((0,), (0,)), ((), ())),
                    preferred_element_type=F32)
                nst[h:h + 1, :] = decay * n_row + jnp.sum(w_col * hd["k_h"], axis=0, keepdims=True)

                hm = _sigmoid(proj[s][self.rows, C_MO + hc:C_MO + hc + ML_HEAD_DIM]) * h_val
                hm = hm * lax.rsqrt(jnp.mean(hm * hm, axis=-1, keepdims=True) + EPS)
                hm = hm * mog_ref[:, hc:hc + ML_HEAD_DIM]
                ml = hm * _silu(proj[s][self.rows, C_MZ + hc:C_MZ + hc + ML_HEAD_DIM])
                mixed[s][rr:rr + CHUNK, X_ML + hc:X_ML + hc + ML_HEAD_DIM] = ml.astype(BF16)

        def conv_taps(self, residues):
            acc = self.conv_acc
            if acc is None:
                acc = jnp.broadcast_to(dwb_ref[...], (CHUNK, D_CONV))
            uwin = extu[pl.ds(self.r0, CHUNK + U_PAD), :]
            first = U_PAD - (CONV_WIDTH - 1)
            for res in residues:
                shifted = _shift_rows(uwin, res)
                for o_ in range(first, U_PAD + 1):
                    if o_ % 8 == res:
                        tap = o_ - first
                        acc = acc + dww_ref[tap:tap + 1, :] * shifted[o_ - res:o_ - res + CHUNK]
            self.conv_acc = acc

        def conv_out(self):
            s, rr = self.s, self.rr
            acc = self.conv_acc
            mu = jnp.mean(acc, axis=-1, keepdims=True)
            xc = acc - mu
            y = xc * lax.rsqrt(jnp.mean(xc * xc, axis=-1, keepdims=True) + EPS)
            y = _silu(y * lng_ref[...] + lnb_ref[...])
            cv = y * _silu(proj[s][self.rows, C_CZ:C_CZ + D_CONV])
            mixed[s][rr:rr + CHUNK, X_CV:X_CV + D_CONV] = cv.astype(BF16)

    def commit_staged(p, s):
        kbuf[pl.ds(_aligned(ATT_LEFT + p * BLOCK, BLOCK), BLOCK), :] = kst[s][...]
        vbuf[pl.ds(_aligned(ATT_LEFT + p * BLOCK, BLOCK), BLOCK), :] = vst[s][...]
        extu[pl.ds(_aligned(U_PAD + p * BLOCK, U_PAD), BLOCK), :] = ust[s][...]

    def stage_c_pieces(p, s):
        rows = pl.ds(_aligned(p * BLOCK, BLOCK), BLOCK)

        def piece(c0):
            def run():
                o_ref[rows, c0:c0 + PIECE] = x_ref[rows, c0:c0 + PIECE] + jnp.dot(
                    mixed[s][...], wout_ref[:, c0:c0 + PIECE], preferred_element_type=F32)
            return run

        return [piece(c0) for c0 in range(0, D_MODEL, PIECE)]

    def maybe_set_pair_bias(p):
        @pl.when(t * tile + p * BLOCK <= ATT_LEFT)
        def _():
            for j in range(2 * BLOCK // CHUNK):
                first_valid = ATT_LEFT - (t * tile + p * BLOCK + j * CHUNK)
                bias_c[j] = bias_ref[...] + jnp.where(key_col >= first_valid, 0.0, NEG)

    @pl.when(jnp.logical_and(b == 0, t == 0))
    def _first_step():
        proj1[BLOCK:BLOCK + ROW_PAD, C_MQ:C_MV] = jnp.zeros((ROW_PAD, C_MV - C_MQ), F32)
        mixed1[...] = jnp.zeros(mixed1.shape, BF16)
        begin, pieces = stage_a_pieces(x_ref[0:BLOCK, :], 0)
        begin()
        for f in pieces:
            f()

    @pl.when(t > 0)
    def _carry():
        extu[0:U_PAD, :] = extu[tile:tile + U_PAD, :]
        kbuf[0:ATT_LEFT, :] = kbuf[tile:tile + ATT_LEFT, :]
        vbuf[0:ATT_LEFT, :] = vbuf[tile:tile + ATT_LEFT, :]

    @pl.when(t == 0)
    def _reset():
        proj0[0:ROW_PAD, C_MQ:C_MV] = jnp.zeros((ROW_PAD, C_MV - C_MQ), F32)
        kbuf[0:ATT_LEFT, :] = jnp.zeros((ATT_LEFT, D_ATT), BF16)
        vbuf[0:ATT_LEFT, :] = jnp.zeros((ATT_LEFT, D_ATT), BF16)
        extu[0:U_PAD, :] = jnp.zeros((U_PAD, D_CONV), F32)
        ct[...] = jnp.zeros(ct.shape, F32)
        nst[...] = jnp.zeros(nst.shape, F32)
        mst[...] = jnp.zeros(mst.shape, F32)

    def half_step(p, s, x_next, p_out):
        begin, a = stage_a_pieces(x_next, 1 - s)
        fillers = stage_c_pieces(p_out, 1 - s) + a
        units = []
        for k in [Chunk(p, s, j) for j in range(CHUNKS_PER_BLOCK)]:
            part = functools.partial
            units += [k.scores, part(k.ml_phase1, (0,)), part(k.ml_phase1, (1,)),
                      part(k.ml_phase1, (2,)), part(k.ml_phase1, (3,)), part(k.conv_taps, (0, 1)),
                      k.softmax_pv, part(k.conv_taps, (2, 3)), part(k.ml_phase2, (0, 1)),
                      part(k.ml_phase2, (2, 3)), part(k.conv_taps, (4, 5)),
                      part(k.ml_phase3, (0, 1)), part(k.ml_phase3, (2, 3)),
                      part(k.conv_taps, (6, 7)), k.conv_out]
        begin()
        commit_staged(p, s)
        fillers.pop(0)()
        n_f = len(fillers)
        for i, unit in enumerate(units):
            unit()
            while fillers and len(fillers) > n_f - (i + 1) * n_f // len(units):
                fillers.pop(0)()

    def block_rows(q):
        return x_ref[pl.ds(_aligned(q * BLOCK, BLOCK), BLOCK), :]

    def body(i, carry):
        p = 2 * i
        maybe_set_pair_bias(p)
        half_step(p, 0, block_rows(p + 1), jnp.maximum(p - 1, 0))
        half_step(p + 1, 1, block_rows(p + 2), p)
        return carry

    if n_blocks > 2:
        lax.fori_loop(0, n_blocks // 2 - 1, body, 0)
    maybe_set_pair_bias(n_blocks - 2)
    half_step(n_blocks - 2, 0, block_rows(n_blocks - 1), max(n_blocks - 3, 0))
    half_step(n_blocks - 1, 1, xn_ref[...], n_blocks - 2)
    for f in stage_c_pieces(n_blocks - 1, 1):
        f()


def _const_spec(shape):
    nd = len(shape)
    return pl.BlockSpec(shape, lambda b, t: (0,) * nd)


def _layer(x, ng, w, wg, gb, aqg, akg, bias, cw, cb, mog, dww, dwb, lng, lnb, wout, *, tile):
    B, S, D = x.shape
    assert D == D_MODEL and S % tile == 0 and tile % (2 * BLOCK) == 0 and tile >= ATT_LEFT
    n_tiles = S // tile
    blocks_per_tile = tile // BLOCK
    consts = (ng, w, wg, gb, aqg, akg, bias, cw, cb, mog, dww, dwb, lng, lnb, wout)
    x_spec = pl.BlockSpec((None, tile, D), lambda b, t: (b, t, 0))

    def next_block(b, t):
        flat = jnp.minimum(b * n_tiles + t + 1, B * n_tiles - 1)
        return (flat // n_tiles, (flat % n_tiles) * blocks_per_tile, 0)

    return pl.pallas_call(
        functools.partial(_layer_kernel, tile=tile),
        grid=(B, n_tiles),
        in_specs=[x_spec, pl.BlockSpec((None, BLOCK, D), next_block)]
        + [_const_spec(c.shape) for c in consts],
        out_specs=x_spec,
        out_shape=jax.ShapeDtypeStruct(x.shape, x.dtype),
        scratch_shapes=(
            [pltpu.VMEM((ROW_PAD + BLOCK, D_PROJ), F32)] * 2
            + [pltpu.VMEM((16, BLOCK), F32)] * 2
            + [pltpu.VMEM((BLOCK, D_ATT), BF16)] * 6
            + [pltpu.VMEM((BLOCK, D_CONV), F32)] * 2
            + [pltpu.VMEM((BLOCK, D_MODEL), BF16)] * 2
            + [pltpu.VMEM((BLOCK, D_MODEL), BF16)]
            + [pltpu.VMEM((ATT_LEFT + tile, D_ATT), BF16)] * 2
            + [pltpu.VMEM((U_PAD + tile, D_CONV), F32),
               pltpu.VMEM((2 * BLOCK // CHUNK, ATT_HEADS * CHUNK, ATT_BAND), F32),
               pltpu.VMEM((2 * BLOCK // CHUNK, ATT_HEADS * CHUNK, ATT_BAND), F32),
               pltpu.VMEM((ML_HEADS, ML_HEAD_DIM, ML_HEAD_DIM), F32),
               pltpu.VMEM((8, ML_HEAD_DIM), F32),
               pltpu.VMEM((8, 128), F32)]),
        compiler_params=pltpu.CompilerParams(
            dimension_semantics=("arbitrary", "arbitrary"),
            vmem_limit_bytes=V7X_VMEM_LIMIT),
        name="hybrid_layer",
    )(x, x, *consts)


def _rel_bias_table(rel_bias):
    heads = rel_bias.shape[0]
    n_far = ATT_LEFT - MAX_REL + CHUNK
    n_near = ATT_BAND + CHUNK - 1 - n_far
    far = jnp.broadcast_to(rel_bias[:, 2 * MAX_REL:], (heads, n_far))
    near = rel_bias[:, 2 * MAX_REL - n_near:2 * MAX_REL][:, ::-1]
    base = jnp.concatenate([far, near], axis=1)
    rows = [base[:, CHUNK - 1 - q:CHUNK - 1 - q + ATT_BAND] for q in range(CHUNK)]
    return jnp.stack(rows, axis=1).astype(F32).reshape(heads * CHUNK, ATT_BAND)


def kernel(x, norm_g, w_in, att_q_g, att_k_g, att_rel_bias, ml_qk_conv_w, ml_qk_conv_b, ml_b_i,
           ml_b_f, ml_out_g, cv_dw_w, cv_dw_b, cv_ln_g, cv_ln_b, w_out):
    depth = w_in.shape[0]
    tile = min(SEQ_TILE, x.shape[1])
    g0 = C_CA
    g1 = g0 + 2 * ML_HEADS
    for l in range(depth):
        w = jnp.concatenate([w_in[l][:, :g0], w_in[l][:, g1:]], axis=1).astype(BF16)
        wg = jnp.zeros((16, D_MODEL), F32)
        wg = wg.at[0:ML_HEADS].set(w_in[l][:, g0:g0 + ML_HEADS].T)
        wg = wg.at[8:8 + ML_HEADS].set(w_in[l][:, g0 + ML_HEADS:g1].T)
        gb = jnp.zeros((16, BLOCK), F32)
        gb = gb.at[0:ML_HEADS].set(jnp.broadcast_to(ml_b_i[l][:, None], (ML_HEADS, BLOCK)))
        gb = gb.at[8:8 + ML_HEADS].set(jnp.broadcast_to(ml_b_f[l][:, None], (ML_HEADS, BLOCK)))
        dww = jnp.zeros((32, D_CONV), F32).at[0:CONV_WIDTH].set(cv_dw_w[l])
        x = _layer(
            x, norm_g[l][None, :], w, wg.astype(BF16), gb,
            jnp.tile(att_q_g[l], ATT_HEADS)[None, :], jnp.tile(att_k_g[l], ATT_HEADS)[None, :],
            _rel_bias_table(att_rel_bias[l]) * LOG2E,
            jnp.zeros((8, 2 * D_ML), F32).at[0:ML_CONV].set(ml_qk_conv_w[l]),
            ml_qk_conv_b[l][None, :], ml_out_g[l][None, :],
            dww, cv_dw_b[l][None, :], cv_ln_g[l][None, :], cv_ln_b[l][None, :],
            w_out[l].astype(BF16), tile=tile)
    return x
```
